```python
import math
import jax, jax.numpy as jnp
from jax import lax
import numpy as np

D_MODEL = 2048
BATCH = 2
SEQ = 4096
DEPTH = 2
DEC_BATCH = 8
DEC_SEQ = 8
PAST_LEN = 16384
PAGE_SIZE = 128

MIX_WIDTH = D_MODEL
FOX_HEADS = 8
FOX_HEAD_DIM = MIX_WIDTH // 2 // FOX_HEADS
DIFF_HEADS = 8
DIFF_HEAD_DIM = MIX_WIDTH // 2 // DIFF_HEADS
DIFF_QK_DIM = DIFF_HEAD_DIM // 2
FOX_W = FOX_HEADS * FOX_HEAD_DIM
DIFF_W = DIFF_HEADS * DIFF_HEAD_DIM
IN_COLS = 3 * FOX_W + FOX_HEADS + 3 * DIFF_W
SPLITS = [FOX_W, 2 * FOX_W, 3 * FOX_W, 3 * FOX_W + FOX_HEADS,
          3 * FOX_W + FOX_HEADS + DIFF_W, 3 * FOX_W + FOX_HEADS + 2 * DIFF_W]
D_FF = 4 * D_MODEL
ROPE_THETA = 10000.0
Q_BLOCK = 128
EPS = 1e-6
FORGET_BIAS_INIT = 3.0

kernel_name = "hymba_fox_diffattn_decoder_step"


def _rmsnorm(x, g):
    xf = x.astype(jnp.float32)
    y = xf * lax.rsqrt(jnp.mean(xf * xf, axis=-1, keepdims=True) + EPS)
    return (y * g.astype(jnp.float32)).astype(x.dtype)


def _rope(x, pos):
    d = x.shape[-1]
    inv_freq = ROPE_THETA ** (-jnp.arange(0, d, 2, dtype=jnp.float32) / d)
    ang = pos.astype(jnp.float32)[:, None] * inv_freq[None, :]
    shape = (1, pos.shape[0]) + (1,) * (x.ndim - 3) + (d // 2,)
    cos = jnp.cos(ang).reshape(shape)
    sin = jnp.sin(ang).reshape(shape)
    xf = x.astype(jnp.float32)
    x1, x2 = xf[..., : d // 2], xf[..., d // 2:]
    return jnp.concatenate([x1 * cos - x2 * sin, x2 * cos + x1 * sin], axis=-1).astype(x.dtype)


def _swept(fn, *q_side):
    tq = q_side[0].shape[1]
    blk = Q_BLOCK if tq % Q_BLOCK == 0 else tq
    nb = tq // blk
    xs = tuple(a.reshape((a.shape[0], nb, blk) + a.shape[2:]).swapaxes(0, 1) for a in q_side)
    out = lax.map(lambda t: fn(*t), xs)
    out = out.swapaxes(0, 1)
    return out.reshape((out.shape[0], tq) + out.shape[3:])


def _fox_attend(q, k, v, q_cum, k_cum, q_pos, k_pos):
    scale = FOX_HEAD_DIM ** -0.5
    k_bias = k_cum.astype(jnp.float32).transpose(0, 2, 1)[:, :, None, :]

    def block(qb, qcb, qpb):
        s = jnp.einsum("bqhd,bkhd->bhqk", qb, k).astype(jnp.float32) * scale
        s = s + qcb.astype(jnp.float32).transpose(0, 2, 1)[..., None] - k_bias
        s = jnp.where(k_pos[None, :] <= qpb[0][:, None], s, -jnp.inf)
        p = jax.nn.softmax(s, axis=-1)
        return jnp.einsum("bhqk,bkhd->bqhd", p.astype(v.dtype), v)

    return _swept(block, q, q_cum, q_pos[None, :])


def _diff_attend(q, k, v, lam, q_pos, k_pos):
    scale = DIFF_QK_DIM ** -0.5

    def block(qb, qpb):
        s = jnp.einsum("bqhcd,bkhcd->bhcqk", qb, k).astype(jnp.float32) * scale
        s = jnp.where(k_pos[None, :] <= qpb[0][:, None], s, -jnp.inf)
        p = jax.nn.softmax(s, axis=-1)
        a = p[:, :, 0] - lam * p[:, :, 1]
        return jnp.einsum("bhqk,bkhd->bqhd", a.astype(v.dtype), v)

    return _swept(block, q, q_pos[None, :])


def _token_mixers(hn, pos, layer, past, w_in, b_forget, fox_q_norm, fox_k_norm,
                  diff_q_norm, diff_k_norm, lq1, lk1, lq2, lk2, diff_subln, w_o):
    B, T, _ = hn.shape
    proj = hn @ w_in
    fq, fk, fv, ff, dq, dk, dv = jnp.split(proj, SPLITS, axis=-1)
    fq = _rmsnorm(fq.reshape(B, T, FOX_HEADS, FOX_HEAD_DIM), fox_q_norm)
    fk = _rmsnorm(fk.reshape(B, T, FOX_HEADS, FOX_HEAD_DIM), fox_k_norm)
    fv = fv.reshape(B, T, FOX_HEADS, FOX_HEAD_DIM)
    logf = jax.nn.log_sigmoid((ff + b_forget).astype(jnp.float32))
    dq = _rope(_rmsnorm(dq.reshape(B, T, DIFF_HEADS, 2, DIFF_QK_DIM), diff_q_norm), pos)
    dk = _rope(_rmsnorm(dk.reshape(B, T, DIFF_HEADS, 2, DIFF_QK_DIM), diff_k_norm), pos)
    dv = dv.reshape(B, T, DIFF_HEADS, DIFF_HEAD_DIM)

    lam_init = 0.8 - 0.6 * math.exp(-0.3 * layer)
    lam = (jnp.exp(jnp.sum(lq1.astype(jnp.float32) * lk1.astype(jnp.float32)))
           - jnp.exp(jnp.sum(lq2.astype(jnp.float32) * lk2.astype(jnp.float32))) + lam_init)

    q_cum = lax.cumsum(logf, axis=1)
    if past is None:
        k_fk, k_fv, k_cum, k_dk, k_dv, k_pos = fk, fv, q_cum, dk, dv, pos
    else:
        pk, pv, plogf, pdk, pdv = past
        plogf = plogf.astype(jnp.float32)
        r = lax.cumsum(plogf, axis=1, reverse=True) - plogf
        k_cum = jnp.concatenate([-r, q_cum], axis=1)
        k_fk = jnp.concatenate([pk, fk], axis=1)
        k_fv = jnp.concatenate([pv, fv], axis=1)
        k_dk = jnp.concatenate([pdk, dk], axis=1)
        k_dv = jnp.concatenate([pdv, dv], axis=1)
        k_pos = jnp.arange(pk.shape[1] + T, dtype=jnp.int32)

    fox_out = _fox_attend(fq, k_fk, k_fv, q_cum, k_cum, pos, k_pos)
    diff_out = _diff_attend(dq, k_dk, k_dv, lam, pos, k_pos)
    diff_out = _rmsnorm(diff_out, diff_subln) * (1.0 - lam_init)
    mix = jnp.concatenate([fox_out.reshape(B, T, FOX_W), diff_out.reshape(B, T, DIFF_W)], axis=-1)
    return mix @ w_o, (fk, fv, logf, dk, dv)


def _trunk(x, pos, past, norm_attn, w_in, b_forget, fox_q_norm, fox_k_norm, diff_q_norm,
           diff_k_norm, lambda_q1, lambda_k1, lambda_q2, lambda_k2, diff_subln, w_o,
           norm_mlp, w_up, w_down):
    news = []
    for l in range(DEPTH):
        h = _rmsnorm(x, norm_attn[l])
        mix, new = _token_mixers(h, pos, l, None if past is None else past[l],
                                 w_in[l], b_forget[l], fox_q_norm[l], fox_k_norm[l],
                                 diff_q_norm[l], diff_k_norm[l], lambda_q1[l], lambda_k1[l],
                                 lambda_q2[l], lambda_k2[l], diff_subln[l], w_o[l])
        x = x + mix
        h = _rmsnorm(x, norm_mlp[l])
        x = x + jnp.square(jax.nn.relu(h @ w_up[l])) @ w_down[l]
        news.append(new)
    fk = jnp.stack([n[0] for n in news])
    fv = jnp.stack([n[1] for n in news])
    flogf = jnp.stack([n[2] for n in news])
    dk = jnp.stack([n[3] for n in news])
    dv = jnp.stack([n[4] for n in news])
    return x, fk, fv, flogf, dk, dv


def setup_inputs(seed: int = 0) -> dict:
    key = jax.random.key(seed)
    ks = jax.random.split(key, 24)
    f32 = jnp.float32
    n_pages = PAST_LEN // PAGE_SIZE
    n_used = DEC_BATCH * n_pages
    n_phys = (5 * n_used + 3) // 4

    def nrm(k, shape, scale=1.0):
        return scale * jax.random.normal(k, shape, f32)

    def gain(k, shape):
        return 1.0 + nrm(k, shape, 0.02)

    page_table = jax.random.permutation(ks[7], n_phys)[:n_used].reshape(DEC_BATCH, n_pages).astype(jnp.int32)
    return {
        "x_prompt": nrm(ks[0], (BATCH, SEQ, D_MODEL)),
        "x_sample": nrm(ks[1], (DEC_BATCH, DEC_SEQ, D_MODEL)),
        "cache_fox_k": nrm(ks[2], (DEPTH, n_phys, PAGE_SIZE, FOX_HEADS, FOX_HEAD_DIM)),
        "cache_fox_v": nrm(ks[3], (DEPTH, n_phys, PAGE_SIZE, FOX_HEADS, FOX_HEAD_DIM)),
        "cache_fox_logf": jax.nn.log_sigmoid(FORGET_BIAS_INIT + nrm(ks[4], (DEPTH, n_phys, PAGE_SIZE, FOX_HEADS))),
        "cache_diff_k": nrm(ks[5], (DEPTH, n_phys, PAGE_SIZE, DIFF_HEADS, 2, DIFF_QK_DIM)),
        "cache_diff_v": nrm(ks[6], (DEPTH, n_phys, PAGE_SIZE, DIFF_HEADS, DIFF_HEAD_DIM)),
        "page_table": page_table,
        "norm_attn": gain(ks[8], (DEPTH, D_MODEL)),
        "w_in": nrm(ks[9], (DEPTH, D_MODEL, IN_COLS), D_MODEL ** -0.5),
        "b_forget": FORGET_BIAS_INIT + nrm(ks[10], (DEPTH, FOX_HEADS), 0.1),
        "fox_q_norm": gain(ks[11], (DEPTH, FOX_HEAD_DIM)),
        "fox_k_norm": gain(ks[12], (DEPTH, FOX_HEAD_DIM)),
        "diff_q_norm": gain(ks[13], (DEPTH, DIFF_QK_DIM)),
        "diff_k_norm": gain(ks[14], (DEPTH, DIFF_QK_DIM)),
        "lambda_q1": nrm(ks[15], (DEPTH, DIFF_QK_DIM), 0.1),
        "lambda_k1": nrm(ks[16], (DEPTH, DIFF_QK_DIM), 0.1),
        "lambda_q2": nrm(ks[17], (DEPTH, DIFF_QK_DIM), 0.1),
        "lambda_k2": nrm(ks[18], (DEPTH, DIFF_QK_DIM), 0.1),
        "diff_subln": gain(ks[19], (DEPTH, DIFF_HEAD_DIM)),
        "w_o": nrm(ks[20], (DEPTH, MIX_WIDTH, D_MODEL), MIX_WIDTH ** -0.5),
        "norm_mlp": gain(ks[21], (DEPTH, D_MODEL)),
        "w_up": nrm(ks[22], (DEPTH, D_MODEL, D_FF), D_MODEL ** -0.5),
        "w_down": nrm(ks[23], (DEPTH, D_FF, D_MODEL), D_FF ** -0.5),
    }


def reference(x_prompt, x_sample, cache_fox_k, cache_fox_v, cache_fox_logf, cache_diff_k,
              cache_diff_v, page_table, norm_attn, w_in, b_forget, fox_q_norm, fox_k_norm,
              diff_q_norm, diff_k_norm, lambda_q1, lambda_k1, lambda_q2, lambda_k2,
              diff_subln, w_o, norm_mlp, w_up, w_down):
    n_pages = page_table.shape[1]

    def gather(cache, l):
        g = cache[l][page_table]
        return g.reshape((g.shape[0], n_pages * g.shape[2]) + g.shape[3:])

    past = [(gather(cache_fox_k, l), gather(cache_fox_v, l), gather(cache_fox_logf, l),
             gather(cache_diff_k, l), gather(cache_diff_v, l)) for l in range(DEPTH)]

    t_prompt = x_prompt.shape[1]
    t_sample = x_sample.shape[1]
    past_len = n_pages * cache_fox_k.shape[2]
    pos_prompt = jnp.arange(t_prompt, dtype=jnp.int32)
    pos_sample = past_len + jnp.arange(t_sample, dtype=jnp.int32)

    y_prompt, p_fox_k, p_fox_v, p_fox_logf, p_diff_k, p_diff_v = _trunk(
        x_prompt, pos_prompt, None, norm_attn, w_in, b_forget, fox_q_norm, fox_k_norm,
        diff_q_norm, diff_k_norm, lambda_q1, lambda_k1, lambda_q2, lambda_k2, diff_subln,
        w_o, norm_mlp, w_up, w_down)
    y_sample, s_fox_k, s_fox_v, s_fox_logf, s_diff_k, s_diff_v = _trunk(
        x_sample, pos_sample, past, norm_attn, w_in, b_forget, fox_q_norm, fox_k_norm,
        diff_q_norm, diff_k_norm, lambda_q1, lambda_k1, lambda_q2, lambda_k2, diff_subln,
        w_o, norm_mlp, w_up, w_down)
    return (y_prompt, y_sample, p_fox_k, p_fox_v, p_fox_logf, p_diff_k, p_diff_v,
            s_fox_k, s_fox_v, s_fox_logf, s_diff_k, s_diff_v)
```

```python
import functools
import math

import jax
import jax.numpy as jnp
from jax import lax
from jax.experimental import pallas as pl
from jax.experimental.pallas import tpu as pltpu

F32 = jnp.float32
BF16 = jnp.bfloat16
EPS = 1e-6
ROPE_THETA = 10000.0
LANES = 128
MIB = 1024 * 1024
NEG_INF = float("-inf")


def _params(sem, vmem_mib):
    return pltpu.CompilerParams(dimension_semantics=sem, vmem_limit_bytes=vmem_mib * MIB)


def _split3(x):
    hi = x.astype(BF16)
    r1 = x - hi.astype(F32)
    mid = r1.astype(BF16)
    lo = (r1 - mid.astype(F32)).astype(BF16)
    return hi, mid, lo


def _dot(a, b):
    return jnp.dot(a, b, preferred_element_type=F32)


def _dot_nt(a, b):
    return lax.dot_general(a, b, (((1,), (1,)), ((), ())), preferred_element_type=F32)


def _dot3(x, w):
    hi, mid, lo = _split3(x)
    return _dot(hi, w) + _dot(mid, w) + _dot(lo, w)


def _dot3_left(w, x):
    hi, mid, lo = _split3(x)
    return _dot(w, hi) + _dot(w, mid) + _dot(w, lo)


def _log_sigmoid(x):
    return jnp.minimum(x, 0.0) - jnp.log1p(jnp.exp(-jnp.abs(x)))


def _rms_rows(x, g):
    ms = jnp.mean(x * x, axis=-1, keepdims=True)
    return x * lax.rsqrt(ms + EPS) * g


def _proj_kernel(*refs, kind, n_heads, scale, q_dtype):
    if kind == "fox":
        (x_ref, g_ref, w_ref, qn_ref, kn_ref, wf_ref, bf_ref,
         q_out, k_out, kb_out, v_out, vb_out, logf_out, xn_sc) = refs
    else:
        (x_ref, g_ref, w_ref, qn_ref, kn_ref, cos_ref, sin_ref,
         q_out, k_out, kb_out, v_out, vb_out, xn_sc) = refs
    j = pl.program_id(1)

    @pl.when(j == 0)
    def _():
        xn_sc[...] = _rms_rows(x_ref[...], g_ref[...]).astype(BF16)
        if kind == "fox":
            ff = _dot(xn_sc[...], wf_ref[...]) + bf_ref[...]
            logf_out[...] = _log_sigmoid(ff)

    acc = _dot(xn_sc[...], w_ref[...])

    def head(h):
        return acc[:, h * LANES:(h + 1) * LANES]

    if kind == "fox":
        def normed(blk, gain):
            return _rms_rows(blk, gain)
    else:
        lane = lax.broadcasted_iota(jnp.int32, (acc.shape[0], LANES), 1)
        lane_lo = lane < (LANES // 2)
        rot_up = (lane % (LANES // 2)) < (LANES // 4)
        cos = cos_ref[...]
        sin = sin_ref[...]

        def normed(blk, gain):
            sq = blk * blk
            s_lo = jnp.sum(jnp.where(lane_lo, sq, 0.0), axis=-1, keepdims=True)
            s_hi = jnp.sum(jnp.where(lane_lo, 0.0, sq), axis=-1, keepdims=True)
            ms = jnp.where(lane_lo, s_lo, s_hi) * (2.0 / LANES)
            y = blk * lax.rsqrt(ms + EPS) * gain
            up = pltpu.roll(y, LANES - LANES // 4, axis=1)
            dn = pltpu.roll(y, LANES // 4, axis=1)
            return y * cos + jnp.where(rot_up, up, dn) * sin

    @pl.when(j == 0)
    def _():
        gain = qn_ref[...]
        for h in range(n_heads):
            q_out[:, h * LANES:(h + 1) * LANES] = (normed(head(h), gain) * scale).astype(q_dtype)

    @pl.when(j == 1)
    def _():
        gain = kn_ref[...]
        for h in range(n_heads):
            y = normed(head(h), gain)
            k_out[:, h * LANES:(h + 1) * LANES] = y
            kb_out[:, h * LANES:(h + 1) * LANES] = y.astype(BF16)

    @pl.when(j == 2)
    def _():
        v_out[...] = acc
        vb_out[...] = acc.astype(BF16)


def _proj(kind, x, g, w3, qn, kn, extra, *, tm, scale, q_dtype):
    m, d = x.shape
    w = w3.shape[1] // 3
    n_heads = w // LANES
    row = lambda i, j: (i, 0)
    const = lambda i, j: (0, 0)
    in_specs = [
        pl.BlockSpec((tm, d), row),
        pl.BlockSpec((1, d), const),
        pl.BlockSpec((d, w), lambda i, j: (0, j)),
        pl.BlockSpec((1, LANES), const),
        pl.BlockSpec((1, LANES), const),
    ]
    out_shape = [
        jax.ShapeDtypeStruct((m, w), q_dtype),
        jax.ShapeDtypeStruct((m, w), F32),
        jax.ShapeDtypeStruct((m, w), BF16),
        jax.ShapeDtypeStruct((m, w), F32),
        jax.ShapeDtypeStruct((m, w), BF16),
    ]
    out_specs = [pl.BlockSpec((tm, w), row) for _ in range(5)]
    if kind == "fox":
        wf, bf = extra
        in_specs += [pl.BlockSpec((d, LANES), const), pl.BlockSpec((1, LANES), const)]
        out_shape.append(jax.ShapeDtypeStruct((m, LANES), F32))
        out_specs.append(pl.BlockSpec((tm, LANES), row))
    else:
        cos, sin = extra
        nb = cos.shape[0] // tm
        tab = lambda i, j: (i % nb, 0)
        in_specs += [pl.BlockSpec((tm, LANES), tab), pl.BlockSpec((tm, LANES), tab)]
    return pl.pallas_call(
        functools.partial(_proj_kernel, kind=kind, n_heads=n_heads, scale=scale, q_dtype=q_dtype),
        grid=(m // tm, 3),
        in_specs=in_specs,
        out_specs=out_specs,
        out_shape=out_shape,
        scratch_shapes=[pltpu.VMEM((tm, d), BF16)],
        compiler_params=_params(("parallel", "arbitrary"), 56),
    )(x, g, w3, qn, kn, *extra)


def _cumaug_kernel(logf_ref, tri_ref, selq_ref, selk_ref, oneq_ref, onek_ref,
                   qa_out, ka_out, carry_sc):
    @pl.when(pl.program_id(1) == 0)
    def _():
        carry_sc[...] = jnp.zeros_like(carry_sc)

    cum = _dot3_left(tri_ref[...], logf_ref[...]) + carry_sc[0:1, :]
    carry_sc[0:1, :] = cum[cum.shape[0] - 1:, :]
    pieces = jnp.concatenate(_split3(cum), axis=1)
    qa_out[...] = (_dot(pieces, selq_ref[...]) + oneq_ref[...]).astype(BF16)
    ka_out[...] = (_dot(pieces, selk_ref[...]) + onek_ref[...]).astype(BF16)


def _aug_constants(n_heads):
    w = n_heads * LANES
    r = jnp.arange(3 * LANES)
    c = jnp.arange(w)
    piece, head = r // LANES, r % LANES
    col_head, col_j = c // LANES, c % LANES
    same = (head[:, None] == col_head[None, :]) & (head[:, None] < n_heads)
    selq = jnp.where(same & (col_j[None, :] == piece[:, None]), 1.0, 0.0).astype(BF16)
    selk = jnp.where(same & (col_j[None, :] == piece[:, None] + 3), -1.0, 0.0).astype(BF16)
    oneq = jnp.where((col_j >= 3) & (col_j < 6), 1.0, 0.0).astype(F32)[None, :]
    onek = jnp.where(col_j < 3, 1.0, 0.0).astype(F32)[None, :]
    return selq, selk, oneq, onek


def _cumaug(logf, batch, n_heads, *, tc):
    m = logf.shape[0]
    t = m // batch
    nt = t // tc
    w = n_heads * LANES
    tri = (jnp.arange(tc)[:, None] >= jnp.arange(tc)[None, :]).astype(BF16)
    selq, selk, oneq, onek = _aug_constants(n_heads)
    const = lambda b, i: (0, 0)
    row = lambda b, i: (b * nt + i, 0)
    return pl.pallas_call(
        _cumaug_kernel,
        grid=(batch, nt),
        in_specs=[
            pl.BlockSpec((tc, LANES), row),
            pl.BlockSpec((tc, tc), const),
            pl.BlockSpec((3 * LANES, w), const),
            pl.BlockSpec((3 * LANES, w), const),
            pl.BlockSpec((1, w), const),
            pl.BlockSpec((1, w), const),
        ],
        out_specs=[pl.BlockSpec((tc, w), row), pl.BlockSpec((tc, w), row)],
        out_shape=[jax.ShapeDtypeStruct((m, w), BF16), jax.ShapeDtypeStruct((m, w), BF16)],
        scratch_shapes=[pltpu.VMEM((8, LANES), F32)],
        compiler_params=_params(("parallel", "arbitrary"), 32),
    )(logf, tri, selq, selk, oneq, onek)


def _lambda(lq1, lk1, lq2, lk2, lam_init):
    a = jnp.sum(lq1[...] * lk1[...], axis=-1, keepdims=True)
    b = jnp.sum(lq2[...] * lk2[...], axis=-1, keepdims=True)
    return jnp.exp(a) - jnp.exp(b) + lam_init


def _flash_kernel(*refs, diff, tq, lam_init):
    if diff:
        (q_ref, k_ref, v_ref, lq1, lk1, lq2, lk2, sub_ref,
         o_ref, q_sc, m_sc, l_sc, acc_sc) = refs
    else:
        (q_ref, qa_ref, k_ref, ka_ref, v_ref,
         o_ref, q_sc, m_sc, l_sc, acc_sc) = refs
    i = pl.program_id(2)
    rows = q_sc.shape[0]

    if diff:
        q = q_ref[...]
        lane = lax.broadcasted_iota(jnp.int32, q.shape, 1)
        zero = jnp.zeros_like(q)
        q_sc[0:tq, :] = jnp.where(lane < LANES // 2, q, zero)
        q_sc[tq:, :] = jnp.where(lane < LANES // 2, zero, q)
    else:
        q_sc[:, 0:LANES] = q_ref[...]
        q_sc[:, LANES:] = qa_ref[...]
    m_sc[...] = jnp.full_like(m_sc, NEG_INF)
    l_sc[...] = jnp.zeros_like(l_sc)
    acc_sc[...] = jnp.zeros_like(acc_sc)

    def step(start, masked):
        k = k_ref[pl.ds(start, tq), :]
        if not diff:
            k = jnp.concatenate([k, ka_ref[pl.ds(start, tq), :]], axis=1)
        s = _dot_nt(q_sc[...], k)
        if masked:
            r = lax.broadcasted_iota(jnp.int32, s.shape, 0)
            c = lax.broadcasted_iota(jnp.int32, s.shape, 1)
            if diff:
                r = r % tq
            s = jnp.where(c <= r, s, NEG_INF)
        m_prev = m_sc[...]
        m_new = jnp.maximum(m_prev, jnp.max(s, axis=-1, keepdims=True))
        alpha = jnp.exp(m_prev - m_new)
        p = jnp.exp(s - m_new)
        l_sc[...] = alpha * l_sc[...] + jnp.sum(p, axis=-1, keepdims=True)
        acc_sc[...] = alpha * acc_sc[...] + _dot(p.astype(BF16), v_ref[pl.ds(start, tq), :])
        m_sc[...] = m_new

    def body(j, carry):
        step(pl.multiple_of(j * tq, tq), False)
        return carry

    lax.fori_loop(0, i, body, 0)
    step(pl.multiple_of(i * tq, tq), True)

    out = acc_sc[...] / l_sc[...]
    if diff:
        lam = _lambda(lq1, lk1, lq2, lk2, lam_init)
        o = out[0:tq, :] - lam * out[tq:, :]
        o = _rms_rows(o, sub_ref[...]) * (1.0 - lam_init)
        o_ref[...] = o.astype(o_ref.dtype)
    else:
        o_ref[...] = out.astype(o_ref.dtype)


def _flash(q, k, v, batch, extra, *, diff, tq, lam_init=0.0):
    m, w = q.shape
    n_heads = w // LANES
    t = m // batch
    nq = t // tq
    qspec = pl.BlockSpec((tq, LANES), lambda b, h, i: (b * nq + i, h))
    kspec = pl.BlockSpec((t, LANES), lambda b, h, i: (b, h))
    if diff:
        vec = pl.BlockSpec((1, LANES // 2), lambda b, h, i: (0, 0))
        sub = pl.BlockSpec((1, LANES), lambda b, h, i: (0, 0))
        in_specs = [qspec, kspec, kspec, vec, vec, vec, vec, sub]
        args = (q, k, v) + tuple(extra)
        rows, qw = 2 * tq, LANES
    else:
        qa, ka = extra
        in_specs = [qspec, qspec, kspec, kspec, kspec]
        args = (q, qa, k, ka, v)
        rows, qw = tq, 2 * LANES
    return pl.pallas_call(
        functools.partial(_flash_kernel, diff=diff, tq=tq, lam_init=lam_init),
        grid=(batch, n_heads, nq),
        in_specs=in_specs,
        out_specs=qspec,
        out_shape=jax.ShapeDtypeStruct((m, w), BF16),
        scratch_shapes=[
            pltpu.VMEM((rows, qw), BF16),
            pltpu.VMEM((rows, 1), F32),
            pltpu.VMEM((rows, 1), F32),
            pltpu.VMEM((rows, LANES), F32),
        ],
        compiler_params=_params(("parallel", "parallel", "arbitrary"), 48),
    )(*args)


def _pastsum_kernel(pt_ref, logf_hbm, maskh_ref, pm_ref, u_ref, ubig_ref, r_out, buf, sem,
                    *, layer, n_pages, n_heads):
    b = pl.program_id(0)

    def copy(p):
        page = pt_ref[b * n_pages + p]
        return pltpu.make_async_copy(logf_hbm.at[layer, page], buf.at[p], sem)

    def start(p, c):
        copy(p).start()
        return c

    def wait(p, c):
        copy(p).wait()
        return c

    lax.fori_loop(0, n_pages, start, 0)
    lax.fori_loop(0, n_pages, wait, 0)

    x = buf[...]
    x = x * maskh_ref[...][None, :, :]
    x = x.reshape(n_pages * n_heads, x.shape[2])
    xt = _dot3(x, pm_ref[...])
    within = _dot3(xt, u_ref[...])
    later = jnp.sum(_dot3_left(ubig_ref[...], xt), axis=1, keepdims=True)
    r_out[...] = within + later


def _pastsum(cache_logf4, page_table_flat, layer, dec_batch, n_heads):
    n_pages = page_table_flat.shape[0] // dec_batch
    th = cache_logf4.shape[3]
    t = th // n_heads
    c = jnp.arange(th)
    maskh = (c[None, :] % n_heads == jnp.arange(n_heads)[:, None]).astype(F32)
    pm = (c[:, None] // n_heads == jnp.arange(t)[None, :]).astype(BF16)
    u = (jnp.arange(t)[:, None] > jnp.arange(t)[None, :]).astype(BF16)
    r = jnp.arange(n_pages * n_heads)
    ubig = ((r[None, :] % n_heads == r[:, None] % n_heads)
            & (r[None, :] // n_heads > r[:, None] // n_heads)).astype(BF16)
    rows = n_pages * n_heads
    const = lambda b, pt: (0, 0)
    grid_spec = pltpu.PrefetchScalarGridSpec(
        num_scalar_prefetch=1,
        grid=(dec_batch,),
        in_specs=[
            pl.BlockSpec(memory_space=pl.ANY),
            pl.BlockSpec((n_heads, th), const),
            pl.BlockSpec((th, t), const),
            pl.BlockSpec((t, t), const),
            pl.BlockSpec((rows, rows), const),
        ],
        out_specs=pl.BlockSpec((None, rows, t), lambda b, pt: (b, 0, 0)),
        scratch_shapes=[pltpu.VMEM((n_pages, 1, th), F32), pltpu.SemaphoreType.DMA(())],
    )
    return pl.pallas_call(
        functools.partial(_pastsum_kernel, layer=layer, n_pages=n_pages, n_heads=n_heads),
        grid_spec=grid_spec,
        out_shape=jax.ShapeDtypeStruct((dec_batch, rows, t), F32),
        compiler_params=_params(("arbitrary",), 32),
    )(page_table_flat, cache_logf4, maskh, pm, u, ubig)


def _decode_kernel(*refs, diff, n_pages, n_heads, t_new, lam_init):
    if diff:
        (pt_ref, q_ref, k_ref, v_ref, kn_ref, vn_ref, lq1, lk1, lq2, lk2, sub_ref,
         o_ref, q_sc, m_sc, l_sc, acc_sc) = refs
    else:
        (pt_ref, q_ref, k_ref, v_ref, kn_ref, vn_ref, r_ref, lw_ref,
         o_ref, q_sc, m_sc, l_sc, acc_sc, qc_sc) = refs
    p = pl.program_id(1)
    rows = q_sc.shape[0]
    groups = 2 * n_heads if diff else n_heads
    group_w = q_sc.shape[1] // groups

    def q_index(shape):
        return lax.broadcasted_iota(jnp.int32, shape, 0) % t_new

    @pl.when(p == 0)
    def _():
        q = jnp.concatenate([q_ref[...]] * (rows // t_new), axis=0)
        r = lax.broadcasted_iota(jnp.int32, q.shape, 0)
        lane = lax.broadcasted_iota(jnp.int32, q.shape, 1)
        if diff:
            grp = ((r // t_new) % n_heads) * 2 + r // (t_new * n_heads)
        else:
            grp = r // t_new
        q_sc[...] = jnp.where(lane // group_w == grp, q, 0.0).astype(BF16)
        m_sc[...] = jnp.full_like(m_sc, NEG_INF)
        l_sc[...] = jnp.zeros_like(l_sc)
        acc_sc[...] = jnp.zeros_like(acc_sc)
        if not diff:
            lw = lw_ref[...]
            i_idx = lax.broadcasted_iota(jnp.int32, lw.shape, 1)
            qc = jnp.sum(jnp.where(i_idx <= q_index(lw.shape), lw, 0.0), axis=1, keepdims=True)
            qc_sc[...] = qc

    def update(s, v_bf16):
        m_prev = m_sc[...]
        m_new = jnp.maximum(m_prev, jnp.max(s, axis=-1, keepdims=True))
        alpha = jnp.exp(m_prev - m_new)
        pr = jnp.exp(s - m_new)
        l_sc[...] = alpha * l_sc[...] + jnp.sum(pr, axis=-1, keepdims=True)
        acc_sc[...] = alpha * acc_sc[...] + _dot(pr.astype(BF16), v_bf16)
        m_sc[...] = m_new

    @pl.when(p < n_pages)
    def _():
        s = _dot_nt(q_sc[...], k_ref[...].astype(BF16))
        if not diff:
            r8 = r_ref[...]
            bias = jnp.concatenate(
                [jnp.broadcast_to(r8[h:h + 1, :], (t_new, r8.shape[1])) for h in range(n_heads)],
                axis=0)
            s = s + (qc_sc[...] + bias)
        update(s, v_ref[...].astype(BF16))

    @pl.when(p == n_pages)
    def _():
        s = _dot_nt(q_sc[...], kn_ref[...].astype(BF16))
        col = lax.broadcasted_iota(jnp.int32, s.shape, 1)
        qi = q_index(s.shape)
        if not diff:
            lw = lw_ref[...]
            bias = jnp.zeros(s.shape, F32)
            for i in range(t_new):
                bias = bias + jnp.where((qi >= i) & (col < i), lw[:, i:i + 1], 0.0)
            s = s + bias
        s = jnp.where(col <= qi, s, NEG_INF)
        update(s, vn_ref[...].astype(BF16))

        acc = acc_sc[...]
        inv_l = 1.0 / l_sc[...]
        if diff:
            lam = _lambda(lq1, lk1, lq2, lk2, lam_init)
            half = n_heads * t_new
        for h in range(n_heads):
            lanes = slice(h * LANES, (h + 1) * LANES)
            r0 = h * t_new
            o = acc[r0:r0 + t_new, lanes] * inv_l[r0:r0 + t_new, :]
            if diff:
                o2 = acc[half + r0:half + r0 + t_new, lanes] * inv_l[half + r0:half + r0 + t_new, :]
                o = _rms_rows(o - lam * o2, sub_ref[...]) * (1.0 - lam_init)
            o_ref[:, lanes] = o


def _decode(q, cache_k, cache_v, k_new, v_new, page_table_flat, layer, dec_batch, extra,
            *, diff, lam_init=0.0):
    w = q.shape[1]
    t_new = q.shape[0] // dec_batch
    page = cache_k.shape[2]
    n_pages = page_table_flat.shape[0] // dec_batch
    n_heads = w // LANES
    rows = (2 if diff else 1) * n_heads * t_new
    last = n_pages - 1

    def page_map(b, p, pt):
        return (layer, pt[b * n_pages + jnp.minimum(p, last)], 0, 0)

    per_b = lambda b, p, pt: (b, 0)
    per_b3 = lambda b, p, pt: (b, 0, 0)
    in_specs = [
        pl.BlockSpec((t_new, w), per_b),
        pl.BlockSpec((None, None, page, w), page_map),
        pl.BlockSpec((None, None, page, w), page_map),
        pl.BlockSpec((None, page, w), per_b3),
        pl.BlockSpec((None, page, w), per_b3),
    ]
    scratch = [
        pltpu.VMEM((rows, w), BF16),
        pltpu.VMEM((rows, 1), F32),
        pltpu.VMEM((rows, 1), F32),
        pltpu.VMEM((rows, w), F32),
    ]
    if diff:
        vec = pl.BlockSpec((1, LANES // 2), lambda b, p, pt: (0, 0))
        in_specs += [vec, vec, vec, vec, pl.BlockSpec((1, LANES), lambda b, p, pt: (0, 0))]
    else:
        in_specs += [
            pl.BlockSpec((None, n_heads, page), lambda b, p, pt: (b, jnp.minimum(p, last), 0)),
            pl.BlockSpec((None, rows, t_new), per_b3),
        ]
        scratch.append(pltpu.VMEM((rows, 1), F32))
    grid_spec = pltpu.PrefetchScalarGridSpec(
        num_scalar_prefetch=1,
        grid=(dec_batch, n_pages + 1),
        in_specs=in_specs,
        out_specs=pl.BlockSpec((t_new, w), per_b),
        scratch_shapes=scratch,
    )
    return pl.pallas_call(
        functools.partial(_decode_kernel, diff=diff, n_pages=n_pages, n_heads=n_heads,
                          t_new=t_new, lam_init=lam_init),
        grid_spec=grid_spec,
        out_shape=jax.ShapeDtypeStruct((dec_batch * t_new, w), F32),
        compiler_params=_params(("parallel", "arbitrary"), 32),
    )(page_table_flat, q, cache_k, cache_v, k_new, v_new, *extra)


def _oproj_kernel(x_ref, a_ref, b_ref, wa_ref, wb_ref, o_ref):
    o_ref[...] = (x_ref[...] + _dot(a_ref[...].astype(BF16), wa_ref[...])
                  + _dot(b_ref[...].astype(BF16), wb_ref[...]))


def _oproj(x, a, b, wa, wb, *, tm):
    m, d = x.shape
    w = a.shape[1]
    row = lambda i: (i, 0)
    const = lambda i: (0, 0)
    return pl.pallas_call(
        _oproj_kernel,
        grid=(m // tm,),
        in_specs=[
            pl.BlockSpec((tm, d), row),
            pl.BlockSpec((tm, w), row),
            pl.BlockSpec((tm, w), row),
            pl.BlockSpec((w, d), const),
            pl.BlockSpec((w, d), const),
        ],
        out_specs=pl.BlockSpec((tm, d), row),
        out_shape=jax.ShapeDtypeStruct((m, d), F32),
        compiler_params=_params(("parallel",), 48),
    )(x, a, b, wa, wb)


def _mlp_kernel(x_ref, g_ref, wu_ref, wd_ref, o_ref, xn_sc, acc_sc):
    j = pl.program_id(1)

    @pl.when(j == 0)
    def _():
        x = x_ref[...]
        xn_sc[...] = _rms_rows(x, g_ref[...]).astype(BF16)
        acc_sc[...] = x

    h = jnp.maximum(_dot(xn_sc[...], wu_ref[...]), 0.0)
    acc_sc[...] += _dot((h * h).astype(BF16), wd_ref[...])

    @pl.when(j == pl.num_programs(1) - 1)
    def _():
        o_ref[...] = acc_sc[...]


def _mlp(x, g, wu, wd, *, tm, tf):
    m, d = x.shape
    f = wu.shape[1]
    row = lambda i, j: (i, 0)
    return pl.pallas_call(
        _mlp_kernel,
        grid=(m // tm, f // tf),
        in_specs=[
            pl.BlockSpec((tm, d), row),
            pl.BlockSpec((1, d), lambda i, j: (0, 0)),
            pl.BlockSpec((d, tf), lambda i, j: (0, j)),
            pl.BlockSpec((tf, d), lambda i, j: (j, 0)),
        ],
        out_specs=pl.BlockSpec((tm, d), row),
        out_shape=jax.ShapeDtypeStruct((m, d), F32),
        scratch_shapes=[pltpu.VMEM((tm, d), BF16), pltpu.VMEM((tm, d), F32)],
        compiler_params=_params(("parallel", "arbitrary"), 48),
    )(x, g, wu, wd)


def _rope_tables(pos, dk):
    inv_freq = ROPE_THETA ** (-jnp.arange(0, dk, 2, dtype=F32) / dk)
    ang = pos.astype(F32)[:, None] * inv_freq[None, :]
    cos, sin = jnp.cos(ang), jnp.sin(ang)
    reps = LANES // dk
    cos = jnp.tile(jnp.concatenate([cos, cos], axis=-1), (1, reps))
    sin = jnp.tile(jnp.concatenate([-sin, sin], axis=-1), (1, reps))
    return cos, sin


def _row_tile(m, target):
    return target if m % target == 0 else m


def kernel(x_prompt, x_sample, cache_fox_k, cache_fox_v, cache_fox_logf, cache_diff_k, cache_diff_v, page_table, norm_attn, w_in, b_forget, fox_q_norm, fox_k_norm, diff_q_norm, diff_k_norm, lambda_q1, lambda_k1, lambda_q2, lambda_k2, diff_subln, w_o, norm_mlp, w_up, w_down):
    depth = w_in.shape[0]
    batch, seq, d_model = x_prompt.shape
    dec_batch, dec_seq, _ = x_sample.shape
    _, n_phys, page, fox_heads, fox_dim = cache_fox_k.shape
    diff_heads, diff_dim = cache_diff_v.shape[3], cache_diff_v.shape[4]
    dk = cache_diff_k.shape[5]
    fox_w = fox_heads * fox_dim
    diff_w = diff_heads * diff_dim
    n_pages = page_table.shape[1]
    past_len = n_pages * page
    assert fox_dim == LANES and diff_dim == LANES and 2 * dk == LANES

    c0 = 3 * fox_w
    c1 = c0 + fox_heads
    w_fox = w_in[:, :, :c0].astype(BF16)
    w_gate = jnp.pad(w_in[:, :, c0:c1], ((0, 0), (0, 0), (0, LANES - fox_heads))).astype(BF16)
    b_gate = jnp.pad(b_forget, ((0, 0), (0, LANES - fox_heads)))[:, None, :]
    w_diff = w_in[:, :, c1:].astype(BF16)
    w_o_fox = w_o[:, :fox_w, :].astype(BF16)
    w_o_diff = w_o[:, fox_w:, :].astype(BF16)
    w_up_b = w_up.astype(BF16)
    w_down_b = w_down.astype(BF16)
    dqn = jnp.tile(diff_q_norm, (1, 2))[:, None, :]
    dkn = jnp.tile(diff_k_norm, (1, 2))[:, None, :]

    pt_flat = page_table.reshape(-1)
    cache_fk = cache_fox_k.reshape(depth, n_phys, page, fox_w)
    cache_fv = cache_fox_v.reshape(depth, n_phys, page, fox_w)
    cache_dk = cache_diff_k.reshape(depth, n_phys, page, diff_w)
    cache_dv = cache_diff_v.reshape(depth, n_phys, page, diff_w)
    cache_lf = cache_fox_logf.reshape(depth, n_phys, 1, page * fox_heads)

    fox_scale = fox_dim ** -0.5
    diff_scale = dk ** -0.5

    def trunk(x, pos, decode):
        m = x.shape[0]
        n_seq = dec_batch if decode else batch
        t = m // n_seq
        tm = _row_tile(m, 512)
        cos, sin = _rope_tables(pos, dk)
        if decode:
            cos, sin = jnp.tile(cos, (n_seq, 1)), jnp.tile(sin, (n_seq, 1))
        q_dtype = F32 if decode else BF16
        news = []
        for l in range(depth):
            lam_init = 0.8 - 0.6 * math.exp(-0.3 * l)
            lam_vecs = (lambda_q1[l][None], lambda_k1[l][None], lambda_q2[l][None],
                        lambda_k2[l][None], diff_subln[l][None])
            fq, fk, fkb, fv, fvb, logf = _proj(
                "fox", x, norm_attn[l][None], w_fox[l], fox_q_norm[l][None], fox_k_norm[l][None],
                (w_gate[l], b_gate[l]), tm=tm, scale=fox_scale, q_dtype=q_dtype)
            dq, dkk, dkb, dv, dvb = _proj(
                "diff", x, norm_attn[l][None], w_diff[l], dqn[l], dkn[l],
                (cos, sin), tm=tm, scale=diff_scale, q_dtype=q_dtype)
            logf_h = logf[:, :fox_heads]
            if decode:
                pad = lambda a: jnp.pad(a.reshape(n_seq, t, -1), ((0, 0), (0, page - t), (0, 0)))
                rsum = _pastsum(cache_lf, pt_flat, l, n_seq, fox_heads)
                lw = jnp.repeat(logf_h.reshape(n_seq, t, fox_heads).transpose(0, 2, 1), t, axis=1)
                fox_o = _decode(fq, cache_fk, cache_fv, pad(fk), pad(fv), pt_flat, l, n_seq,
                                (rsum, lw), diff=False)
                diff_o = _decode(dq, cache_dk, cache_dv, pad(dkk), pad(dv), pt_flat, l, n_seq,
                                 lam_vecs, diff=True, lam_init=lam_init)
            else:
                qa, ka = _cumaug(logf, n_seq, fox_heads, tc=512)
                fox_o = _flash(fq, fkb, fvb, n_seq, (qa, ka), diff=False, tq=512)
                diff_o = _flash(dq, dkb, dvb, n_seq, lam_vecs, diff=True, tq=512,
                                lam_init=lam_init)
            x = _oproj(x, fox_o, diff_o, w_o_fox[l], w_o_diff[l], tm=tm)
            x = _mlp(x, norm_mlp[l][None], w_up_b[l], w_down_b[l], tm=tm, tf=512)
            news.append((fk, fv, logf_h, dkk, dv))
        stack = lambda i, shape: jnp.stack([n[i] for n in news]).reshape((depth, n_seq, t) + shape)
        return (x.reshape(n_seq, t, d_model),
                stack(0, (fox_heads, fox_dim)), stack(1, (fox_heads, fox_dim)),
                stack(2, (fox_heads,)), stack(3, (diff_heads, 2, dk)),
                stack(4, (diff_heads, diff_dim)))

    pos_prompt = jnp.arange(seq, dtype=jnp.int32)
    pos_sample = past_len + jnp.arange(dec_seq, dtype=jnp.int32)
    yp = trunk(x_prompt.reshape(batch * seq, d_model), pos_prompt, False)
    ys = trunk(x_sample.reshape(dec_batch * dec_seq, d_model), pos_sample, True)
    return (yp[0], ys[0]) + yp[1:] + ys[1:]
```

```python
import functools
import math

import jax
import jax.numpy as jnp
from jax import lax
from jax.experimental import pallas as pl
from jax.experimental.pallas import tpu as pltpu

F32 = jnp.float32
BF16 = jnp.bfloat16
EPS = 1e-6
ROPE_THETA = 10000.0
NEG_INF = float("-inf")
MIB = 1024 * 1024

LANES = 128
MXU_WIDTH = 256
ROW_TILE = 512
MLP_FF_TILE = 512
PAGES_PER_STEP = 8
SUM_ROWS = 16
LOG2E = math.log2(math.e)


def _params(sem, vmem_mib):
    return pltpu.CompilerParams(dimension_semantics=sem, vmem_limit_bytes=vmem_mib * MIB)


def _split3(x):
    hi = x.astype(BF16)
    r1 = x - hi.astype(F32)
    mid = r1.astype(BF16)
    lo = (r1 - mid.astype(F32)).astype(BF16)
    return hi, mid, lo


def _dot(a, b):
    return jnp.dot(a, b, preferred_element_type=F32)


def _dot_nt(a, b):
    return lax.dot_general(a, b, (((1,), (1,)), ((), ())), preferred_element_type=F32)


def _dot3(x, w):
    hi, mid, lo = _split3(x)
    return _dot(hi, w) + _dot(mid, w) + _dot(lo, w)


def _dot3_left(w, x):
    hi, mid, lo = _split3(x)
    return _dot(w, hi) + _dot(w, mid) + _dot(w, lo)


def _log_sigmoid(x):
    return jnp.minimum(x, 0.0) - jnp.log1p(jnp.exp(-jnp.abs(x)))


def _rms_rows(x, g):
    ms = jnp.mean(x * x, axis=-1, keepdims=True)
    return x * lax.rsqrt(ms + EPS) * g


def _proj_kernel(*refs, kind, n_heads, scale, q_dtype, emit_vt):
    refs = list(refs)
    x_ref, g_ref, w_ref, qn_ref, kn_ref, e0_ref, e1_ref = refs[:7]
    q_out, k_out, kb_out, v_out = refs[7:11]
    rest = refs[11:]
    vt_out = rest.pop(0) if emit_vt else None
    logf_out = rest.pop(0) if kind == "fox" else None
    xn_sc = rest.pop(0)
    j = pl.program_id(1)

    @pl.when(j == 0)
    def _():
        xn_sc[...] = _rms_rows(x_ref[...], g_ref[...]).astype(BF16)
        if kind == "fox":
            ff = _dot(xn_sc[...], e0_ref[...]) + e1_ref[...]
            logf_out[...] = _log_sigmoid(ff)

    acc = _dot(xn_sc[...], w_ref[...])

    def head(h):
        return acc[:, h * LANES:(h + 1) * LANES]

    if kind == "fox":
        def normed(blk, gain):
            return _rms_rows(blk, gain)
    else:
        lane = lax.broadcasted_iota(jnp.int32, (acc.shape[0], LANES), 1)
        lane_lo = lane < (LANES // 2)
        rot_up = (lane % (LANES // 2)) < (LANES // 4)
        cos = e0_ref[...]
        sin = e1_ref[...]

        def normed(blk, gain):
            sq = blk * blk
            s_lo = jnp.sum(jnp.where(lane_lo, sq, 0.0), axis=-1, keepdims=True)
            s_hi = jnp.sum(jnp.where(lane_lo, 0.0, sq), axis=-1, keepdims=True)
            ms = jnp.where(lane_lo, s_lo, s_hi) * (2.0 / LANES)
            y = blk * lax.rsqrt(ms + EPS) * gain
            up = pltpu.roll(y, LANES - LANES // 4, axis=1)
            dn = pltpu.roll(y, LANES // 4, axis=1)
            return y * cos + jnp.where(rot_up, up, dn) * sin

    @pl.when(j == 0)
    def _():
        gain = qn_ref[...]
        for h in range(n_heads):
            q_out[:, h * LANES:(h + 1) * LANES] = (normed(head(h), gain) * scale).astype(q_dtype)

    @pl.when(j == 1)
    def _():
        gain = kn_ref[...]
        for h in range(n_heads):
            y = normed(head(h), gain)
            k_out[:, h * LANES:(h + 1) * LANES] = y
            kb_out[:, h * LANES:(h + 1) * LANES] = y.astype(BF16)

    @pl.when(j == 2)
    def _():
        v_out[...] = acc
        if emit_vt:
            vt_out[...] = acc.T.astype(BF16).reshape(vt_out.shape)


def _proj(kind, x, g, w3, qn, kn, extra, *, tm, scale, q_dtype, emit_vt):
    m, d = x.shape
    w = w3.shape[1] // 3
    n_heads = w // LANES
    row = lambda i, j: (i, 0)
    const = lambda i, j: (0, 0)
    in_specs = [
        pl.BlockSpec((tm, d), row),
        pl.BlockSpec((1, d), const),
        pl.BlockSpec((d, w), lambda i, j: (0, j)),
        pl.BlockSpec((1, LANES), const),
        pl.BlockSpec((1, LANES), const),
    ]
    if kind == "fox":
        in_specs += [pl.BlockSpec((d, LANES), const), pl.BlockSpec((1, LANES), const)]
    else:
        nb = extra[0].shape[0] // tm
        tab = lambda i, j: (i % nb, 0)
        in_specs += [pl.BlockSpec((tm, LANES), tab), pl.BlockSpec((tm, LANES), tab)]
    out_shape = [
        jax.ShapeDtypeStruct((m, w), q_dtype),
        jax.ShapeDtypeStruct((m, w), F32),
        jax.ShapeDtypeStruct((m, w), BF16),
        jax.ShapeDtypeStruct((m, w), F32),
    ]
    out_specs = [pl.BlockSpec((tm, w), row) for _ in range(4)]
    if emit_vt:
        out_shape.append(jax.ShapeDtypeStruct((n_heads, m // tm, LANES, tm), BF16))
        out_specs.append(pl.BlockSpec((n_heads, 1, LANES, tm), lambda i, j: (0, i, 0, 0)))
    if kind == "fox":
        out_shape.append(jax.ShapeDtypeStruct((m, LANES), F32))
        out_specs.append(pl.BlockSpec((tm, LANES), row))
    return pl.pallas_call(
        functools.partial(_proj_kernel, kind=kind, n_heads=n_heads, scale=scale,
                          q_dtype=q_dtype, emit_vt=emit_vt),
        grid=(m // tm, 3),
        in_specs=in_specs,
        out_specs=out_specs,
        out_shape=out_shape,
        scratch_shapes=[pltpu.VMEM((tm, d), BF16)],
        compiler_params=_params(("parallel", "arbitrary"), 56),
    )(x, g, w3, qn, kn, *extra)


def _cumaug_kernel(logf_ref, tri_ref, selq_ref, selk_ref, oneq_ref, onek_ref,
                   qa_out, ka_out, carry_sc):
    @pl.when(pl.program_id(1) == 0)
    def _():
        carry_sc[...] = jnp.zeros_like(carry_sc)

    cum = _dot3_left(tri_ref[...], logf_ref[...]) + carry_sc[0:1, :]
    carry_sc[0:1, :] = cum[cum.shape[0] - 1:, :]
    pieces = jnp.concatenate(_split3(cum * LOG2E), axis=1)
    qa_out[...] = (_dot(pieces, selq_ref[...]) + oneq_ref[...]).astype(BF16)
    ka_out[...] = (_dot(pieces, selk_ref[...]) + onek_ref[...]).astype(BF16)


def _aug_constants(n_heads):
    w = n_heads * LANES
    r = jnp.arange(3 * LANES)
    c = jnp.arange(w)
    piece, head = r // LANES, r % LANES
    col_head, col_j = c // LANES, c % LANES
    same = (head[:, None] == col_head[None, :]) & (head[:, None] < n_heads)
    selq = jnp.where(same & (col_j[None, :] == piece[:, None]), 1.0, 0.0).astype(BF16)
    selk = jnp.where(same & (col_j[None, :] == piece[:, None] + 3), -1.0, 0.0).astype(BF16)
    oneq = jnp.where((col_j >= 3) & (col_j < 6), 1.0, 0.0).astype(F32)[None, :]
    onek = jnp.where(col_j < 3, 1.0, 0.0).astype(F32)[None, :]
    return selq, selk, oneq, onek


def _cumaug(logf, batch, n_heads, *, tc):
    m = logf.shape[0]
    t = m // batch
    nt = t // tc
    w = n_heads * LANES
    tri = (jnp.arange(tc)[:, None] >= jnp.arange(tc)[None, :]).astype(BF16)
    selq, selk, oneq, onek = _aug_constants(n_heads)
    const = lambda b, i: (0, 0)
    row = lambda b, i: (b * nt + i, 0)
    return pl.pallas_call(
        _cumaug_kernel,
        grid=(batch, nt),
        in_specs=[
            pl.BlockSpec((tc, LANES), row),
            pl.BlockSpec((tc, tc), const),
            pl.BlockSpec((3 * LANES, w), const),
            pl.BlockSpec((3 * LANES, w), const),
            pl.BlockSpec((1, w), const),
            pl.BlockSpec((1, w), const),
        ],
        out_specs=[pl.BlockSpec((tc, w), row), pl.BlockSpec((tc, w), row)],
        out_shape=[jax.ShapeDtypeStruct((m, w), BF16), jax.ShapeDtypeStruct((m, w), BF16)],
        scratch_shapes=[pltpu.VMEM((8, LANES), F32)],
        compiler_params=_params(("parallel", "arbitrary"), 32),
    )(logf, tri, selq, selk, oneq, onek)


def _lambda(lq1, lk1, lq2, lk2, lam_init):
    a = jnp.sum(lq1[...] * lk1[...], axis=-1, keepdims=True)
    b = jnp.sum(lq2[...] * lk2[...], axis=-1, keepdims=True)
    return jnp.exp(a) - jnp.exp(b) + lam_init


def _flash_kernel(*refs, diff, tq, lam_init):
    if diff:
        (q_ref, k_ref, vt_ref, lq1, lk1, lq2, lk2, sub_ref,
         o_ref, q_sc, m_sc, acc_sc) = refs
    else:
        (q_ref, qa_ref, k_ref, ka_ref, vt_ref,
         o_ref, q_sc, m_sc, acc_sc) = refs
    i = pl.program_id(2)
    cols = q_sc.shape[0]

    if diff:
        q = q_ref[...]
        lane = lax.broadcasted_iota(jnp.int32, q.shape, 1)
        zero = jnp.zeros_like(q)
        q_sc[0:tq, :] = jnp.where(lane < LANES // 2, q, zero)
        q_sc[tq:, :] = jnp.where(lane < LANES // 2, zero, q)
    else:
        q_sc[:, 0:LANES] = q_ref[...]
        q_sc[:, LANES:] = qa_ref[...]
    m_sc[...] = jnp.full_like(m_sc, NEG_INF)
    acc_sc[...] = jnp.zeros_like(acc_sc)
    ones = jnp.ones((SUM_ROWS, tq), BF16)

    def step(j, masked):
        start = pl.multiple_of(j * tq, tq)
        k = k_ref[pl.ds(start, tq), :]
        if not diff:
            k = jnp.concatenate([k, ka_ref[pl.ds(start, tq), :]], axis=1)
        vt = jnp.concatenate([vt_ref[j], ones], axis=0)
        chunks = [slice(c * MXU_WIDTH, (c + 1) * MXU_WIDTH) for c in range(cols // MXU_WIDTH)]
        scores = [_dot_nt(k, q_sc[cs, :]) for cs in chunks]
        for c, cs in enumerate(chunks):
            s = scores[c]
            if masked:
                key = lax.broadcasted_iota(jnp.int32, s.shape, 0)
                qry = (lax.broadcasted_iota(jnp.int32, s.shape, 1) + c * MXU_WIDTH) % tq
                s = jnp.where(key <= qry, s, NEG_INF)
            m_prev = m_sc[:, cs]
            m_new = jnp.maximum(m_prev, jnp.max(s, axis=0, keepdims=True))
            alpha = jnp.exp2(m_prev - m_new)
            p = jnp.exp2(s - m_new).astype(BF16)
            acc_sc[:, cs] = alpha * acc_sc[:, cs] + _dot(vt, p)
            m_sc[:, cs] = m_new

    def body(j, carry):
        step(j, False)
        return carry

    lax.fori_loop(0, i, body, 0)
    step(i, True)

    out_t = acc_sc[0:LANES, :] / acc_sc[LANES:LANES + 1, :]
    if diff:
        lam = _lambda(lq1, lk1, lq2, lk2, lam_init)
        o = out_t[:, 0:tq] - lam * out_t[:, tq:]
        ms = jnp.mean(o * o, axis=0, keepdims=True)
        o = o * lax.rsqrt(ms + EPS) * sub_ref[...] * (1.0 - lam_init)
        o_ref[...] = o.T.astype(o_ref.dtype)
    else:
        o_ref[...] = out_t.T.astype(o_ref.dtype)


def _flash(q, k, vt, batch, extra, *, diff, tq, lam_init=0.0):
    m, w = q.shape
    n_heads = w // LANES
    t = m // batch
    nq = t // tq
    qspec = pl.BlockSpec((tq, LANES), lambda b, h, i: (b * nq + i, h))
    kspec = pl.BlockSpec((t, LANES), lambda b, h, i: (b, h))
    vspec = pl.BlockSpec((None, nq, LANES, tq), lambda b, h, i: (h, b, 0, 0))
    if diff:
        vec = pl.BlockSpec((1, LANES // 2), lambda b, h, i: (0, 0))
        sub = pl.BlockSpec((LANES, 1), lambda b, h, i: (0, 0))
        in_specs = [qspec, kspec, vspec, vec, vec, vec, vec, sub]
        args = (q, k, vt) + tuple(extra)
        cols, qw = 2 * tq, LANES
    else:
        qa, ka = extra
        in_specs = [qspec, qspec, kspec, kspec, vspec]
        args = (q, qa, k, ka, vt)
        cols, qw = tq, 2 * LANES
    return pl.pallas_call(
        functools.partial(_flash_kernel, diff=diff, tq=tq, lam_init=lam_init),
        grid=(batch, n_heads, nq),
        in_specs=in_specs,
        out_specs=qspec,
        out_shape=jax.ShapeDtypeStruct((m, w), BF16),
        scratch_shapes=[
            pltpu.VMEM((cols, qw), BF16),
            pltpu.VMEM((1, cols), F32),
            pltpu.VMEM((LANES + SUM_ROWS, cols), F32),
        ],
        compiler_params=_params(("parallel", "parallel", "arbitrary"), 48),
    )(*args)


def _pastsum_kernel(pt_ref, logf_hbm, u_ref, ubig_ref, r_out, buf, sem, *, layer, n_pages):
    b = pl.program_id(0)

    def copy(p):
        page = pt_ref[b * n_pages + p]
        return pltpu.make_async_copy(logf_hbm.at[layer, page], buf.at[p], sem)

    def start(p, c):
        copy(p).start()
        return c

    def wait(p, c):
        copy(p).wait()
        return c

    lax.fori_loop(0, n_pages, start, 0)
    lax.fori_loop(0, n_pages, wait, 0)

    xt = buf[...].reshape(r_out.shape)
    within = _dot3(xt, u_ref[...])
    later = jnp.sum(_dot3_left(ubig_ref[...], xt), axis=1, keepdims=True)
    r_out[...] = within + later


def _pastsum(cache_logf_t, page_table_flat, layer, dec_batch):
    n_pages = page_table_flat.shape[0] // dec_batch
    n_heads, t = cache_logf_t.shape[2:]
    u = (jnp.arange(t)[:, None] > jnp.arange(t)[None, :]).astype(BF16)
    r = jnp.arange(n_pages * n_heads)
    ubig = ((r[None, :] % n_heads == r[:, None] % n_heads)
            & (r[None, :] // n_heads > r[:, None] // n_heads)).astype(BF16)
    rows = n_pages * n_heads
    const = lambda b, pt: (0, 0)
    grid_spec = pltpu.PrefetchScalarGridSpec(
        num_scalar_prefetch=1,
        grid=(dec_batch,),
        in_specs=[
            pl.BlockSpec(memory_space=pl.ANY),
            pl.BlockSpec((t, t), const),
            pl.BlockSpec((rows, rows), const),
        ],
        out_specs=pl.BlockSpec((None, rows, t), lambda b, pt: (b, 0, 0)),
        scratch_shapes=[pltpu.VMEM((n_pages, n_heads, t), F32), pltpu.SemaphoreType.DMA(())],
    )
    return pl.pallas_call(
        functools.partial(_pastsum_kernel, layer=layer, n_pages=n_pages),
        grid_spec=grid_spec,
        out_shape=jax.ShapeDtypeStruct((dec_batch, rows, t), F32),
        compiler_params=_params(("arbitrary",), 32),
    )(page_table_flat, cache_logf_t, u, ubig)


def _decode_kernel(*refs, diff, n_steps, group, n_heads, t_new, page, lam_init):
    refs = list(refs)
    pt_ref, q_ref = refs[0], refs[1]
    k_refs = refs[2:2 + group]
    v_refs = refs[2 + group:2 + 2 * group]
    rest = refs[2 + 2 * group:]
    if diff:
        (kn_ref, vn_ref, lq1, lk1, lq2, lk2, sub_ref,
         o_ref, q_sc, m_sc, l_sc, acc_sc) = rest
    else:
        (kn_ref, vn_ref, r_ref, lw_ref,
         o_ref, q_sc, m_sc, l_sc, acc_sc, qc_sc) = rest
    p = pl.program_id(1)
    rows = q_sc.shape[0]
    groups = 2 * n_heads if diff else n_heads
    group_w = q_sc.shape[1] // groups

    def q_index(shape):
        return lax.broadcasted_iota(jnp.int32, shape, 0) % t_new

    def page_rows(ref):
        heads = [ref[pl.ds(h, page, stride=n_heads), :] for h in range(n_heads)]
        return jnp.concatenate(heads, axis=1).astype(BF16)

    @pl.when(p == 0)
    def _():
        q = jnp.concatenate([q_ref[...]] * (rows // t_new), axis=0)
        r = lax.broadcasted_iota(jnp.int32, q.shape, 0)
        lane = lax.broadcasted_iota(jnp.int32, q.shape, 1)
        if diff:
            grp = ((r // t_new) % n_heads) * 2 + r // (t_new * n_heads)
        else:
            grp = r // t_new
        q_sc[...] = jnp.where(lane // group_w == grp, q, 0.0).astype(BF16)
        m_sc[...] = jnp.full_like(m_sc, NEG_INF)
        l_sc[...] = jnp.zeros_like(l_sc)
        acc_sc[...] = jnp.zeros_like(acc_sc)
        if not diff:
            lw = lw_ref[...]
            i_idx = lax.broadcasted_iota(jnp.int32, lw.shape, 1)
            qc = jnp.sum(jnp.where(i_idx <= q_index(lw.shape), lw, 0.0), axis=1, keepdims=True)
            qc_sc[...] = qc

    def update(s, values):
        m_prev = m_sc[...]
        m_new = jnp.maximum(m_prev, jnp.max(s, axis=-1, keepdims=True))
        alpha = jnp.exp(m_prev - m_new)
        pr = jnp.exp(s - m_new)
        l_sc[...] = alpha * l_sc[...] + jnp.sum(pr, axis=-1, keepdims=True)
        pr = pr.astype(BF16)
        pv = _dot(pr[:, 0:page], values[0])
        for g in range(1, len(values)):
            pv = pv + _dot(pr[:, g * page:(g + 1) * page], values[g])
        acc_sc[...] = alpha * acc_sc[...] + pv
        m_sc[...] = m_new

    @pl.when(p < n_steps)
    def _():
        parts = []
        for g in range(group):
            if diff:
                s = _dot(q_sc[...], k_refs[g][...].astype(BF16))
            else:
                s = _dot_nt(q_sc[...], page_rows(k_refs[g]))
                r8 = r_ref[g * n_heads:(g + 1) * n_heads, :]
                bias = jnp.concatenate(
                    [jnp.broadcast_to(r8[h:h + 1, :], (t_new, page)) for h in range(n_heads)],
                    axis=0)
                s = s + (qc_sc[...] + bias)
            parts.append(s)
        update(jnp.concatenate(parts, axis=1), [page_rows(v) for v in v_refs])

    @pl.when(p == n_steps)
    def _():
        s = _dot_nt(q_sc[...], kn_ref[...].astype(BF16))
        col = lax.broadcasted_iota(jnp.int32, s.shape, 1)
        qi = q_index(s.shape)
        if not diff:
            lw = lw_ref[...]
            bias = jnp.zeros(s.shape, F32)
            for i in range(t_new):
                bias = bias + jnp.where((qi >= i) & (col < i), lw[:, i:i + 1], 0.0)
            s = s + bias
        s = jnp.where(col <= qi, s, NEG_INF)
        update(s, [vn_ref[...].astype(BF16)])

        acc = acc_sc[...]
        inv_l = 1.0 / l_sc[...]
        if diff:
            lam = _lambda(lq1, lk1, lq2, lk2, lam_init)
            half = n_heads * t_new
        for h in range(n_heads):
            lanes = slice(h * LANES, (h + 1) * LANES)
            r0 = h * t_new
            o = acc[r0:r0 + t_new, lanes] * inv_l[r0:r0 + t_new, :]
            if diff:
                o2 = acc[half + r0:half + r0 + t_new, lanes] * inv_l[half + r0:half + r0 + t_new, :]
                o = _rms_rows(o - lam * o2, sub_ref[...]) * (1.0 - lam_init)
            o_ref[:, lanes] = o


def _decode(q, cache_k, cache_v, k_new, v_new, page_table_flat, layer, dec_batch, extra,
            *, diff, group, lam_init=0.0):
    w = q.shape[1]
    t_new = q.shape[0] // dec_batch
    page = k_new.shape[1]
    n_pages = page_table_flat.shape[0] // dec_batch
    n_steps = n_pages // group
    n_heads = w // LANES
    rows = (2 if diff else 1) * n_heads * t_new

    def page_map(g):
        def index(b, p, pt):
            return (layer, pt[b * n_pages + jnp.minimum(p, n_steps - 1) * group + g], 0, 0)
        return index

    per_b = lambda b, p, pt: (b, 0)
    per_b3 = lambda b, p, pt: (b, 0, 0)
    kblock = (None, None) + cache_k.shape[2:]
    vblock = (None, None) + cache_v.shape[2:]
    in_specs = [pl.BlockSpec((t_new, w), per_b)]
    in_specs += [pl.BlockSpec(kblock, page_map(g)) for g in range(group)]
    in_specs += [pl.BlockSpec(vblock, page_map(g)) for g in range(group)]
    in_specs += [pl.BlockSpec((None, page, w), per_b3), pl.BlockSpec((None, page, w), per_b3)]
    scratch = [
        pltpu.VMEM((rows, w), BF16),
        pltpu.VMEM((rows, 1), F32),
        pltpu.VMEM((rows, 1), F32),
        pltpu.VMEM((rows, w), F32),
    ]
    if diff:
        vec = pl.BlockSpec((1, LANES // 2), lambda b, p, pt: (0, 0))
        in_specs += [vec, vec, vec, vec, pl.BlockSpec((1, LANES), lambda b, p, pt: (0, 0))]
    else:
        in_specs += [
            pl.BlockSpec((None, group * n_heads, page),
                         lambda b, p, pt: (b, jnp.minimum(p, n_steps - 1), 0)),
            pl.BlockSpec((None, rows, t_new), per_b3),
        ]
        scratch.append(pltpu.VMEM((rows, 1), F32))
    grid_spec = pltpu.PrefetchScalarGridSpec(
        num_scalar_prefetch=1,
        grid=(dec_batch, n_steps + 1),
        in_specs=in_specs,
        out_specs=pl.BlockSpec((t_new, w), per_b),
        scratch_shapes=scratch,
    )
    return pl.pallas_call(
        functools.partial(_decode_kernel, diff=diff, n_steps=n_steps, group=group,
                          n_heads=n_heads, t_new=t_new, page=page, lam_init=lam_init),
        grid_spec=grid_spec,
        out_shape=jax.ShapeDtypeStruct((dec_batch * t_new, w), F32),
        compiler_params=_params(("parallel", "arbitrary"), 48),
    )(page_table_flat, q, *([cache_k] * group), *([cache_v] * group), k_new, v_new, *extra)


def _oproj_kernel(x_ref, a_ref, b_ref, wa_ref, wb_ref, o_ref):
    o_ref[...] = (x_ref[...] + _dot(a_ref[...].astype(BF16), wa_ref[...])
                  + _dot(b_ref[...].astype(BF16), wb_ref[...]))


def _oproj(x, a, b, wa, wb, *, tm):
    m, d = x.shape
    w = a.shape[1]
    row = lambda i: (i, 0)
    const = lambda i: (0, 0)
    return pl.pallas_call(
        _oproj_kernel,
        grid=(m // tm,),
        in_specs=[
            pl.BlockSpec((tm, d), row),
            pl.BlockSpec((tm, w), row),
            pl.BlockSpec((tm, w), row),
            pl.BlockSpec((w, d), const),
            pl.BlockSpec((w, d), const),
        ],
        out_specs=pl.BlockSpec((tm, d), row),
        out_shape=jax.ShapeDtypeStruct((m, d), F32),
        compiler_params=_params(("parallel",), 48),
    )(x, a, b, wa, wb)


def _mlp_kernel(x_ref, g_ref, wu_ref, wd_ref, o_ref, xn_sc, acc_sc):
    j = pl.program_id(1)

    @pl.when(j == 0)
    def _():
        x = x_ref[...]
        xn_sc[...] = _rms_rows(x, g_ref[...]).astype(BF16)
        acc_sc[...] = x

    h = jnp.maximum(_dot(xn_sc[...], wu_ref[...]), 0.0)
    acc_sc[...] += _dot((h * h).astype(BF16), wd_ref[...])

    @pl.when(j == pl.num_programs(1) - 1)
    def _():
        o_ref[...] = acc_sc[...]


def _mlp(x, g, wu, wd, *, tm, tf):
    m, d = x.shape
    f = wu.shape[1]
    row = lambda i, j: (i, 0)
    return pl.pallas_call(
        _mlp_kernel,
        grid=(m // tm, f // tf),
        in_specs=[
            pl.BlockSpec((tm, d), row),
            pl.BlockSpec((1, d), lambda i, j: (0, 0)),
            pl.BlockSpec((d, tf), lambda i, j: (0, j)),
            pl.BlockSpec((tf, d), lambda i, j: (j, 0)),
        ],
        out_specs=pl.BlockSpec((tm, d), row),
        out_shape=jax.ShapeDtypeStruct((m, d), F32),
        scratch_shapes=[pltpu.VMEM((tm, d), BF16), pltpu.VMEM((tm, d), F32)],
        compiler_params=_params(("parallel", "arbitrary"), 48),
    )(x, g, wu, wd)


def _rope_tables(pos, dk):
    inv_freq = ROPE_THETA ** (-jnp.arange(0, dk, 2, dtype=F32) / dk)
    ang = pos.astype(F32)[:, None] * inv_freq[None, :]
    cos, sin = jnp.cos(ang), jnp.sin(ang)
    reps = LANES // dk
    cos = jnp.tile(jnp.concatenate([cos, cos], axis=-1), (1, reps))
    sin = jnp.tile(jnp.concatenate([-sin, sin], axis=-1), (1, reps))
    return cos, sin


def _largest_divisor(n, cap):
    return max(d for d in range(1, cap + 1) if n % d == 0)


def kernel(x_prompt, x_sample, cache_fox_k, cache_fox_v, cache_fox_logf, cache_diff_k, cache_diff_v, page_table, norm_attn, w_in, b_forget, fox_q_norm, fox_k_norm, diff_q_norm, diff_k_norm, lambda_q1, lambda_k1, lambda_q2, lambda_k2, diff_subln, w_o, norm_mlp, w_up, w_down):
    depth = w_in.shape[0]
    batch, seq, d_model = x_prompt.shape
    dec_batch, dec_seq, _ = x_sample.shape
    _, n_phys, page, fox_heads, fox_dim = cache_fox_k.shape
    diff_heads, diff_dim = cache_diff_v.shape[3], cache_diff_v.shape[4]
    dk = cache_diff_k.shape[5]
    fox_w = fox_heads * fox_dim
    diff_w = diff_heads * diff_dim
    n_pages = page_table.shape[1]
    past_len = n_pages * page
    assert fox_dim == LANES and diff_dim == LANES and 2 * dk == LANES
    assert seq % ROW_TILE == 0

    c0 = 3 * fox_w
    c1 = c0 + fox_heads
    w_fox = w_in[:, :, :c0].astype(BF16)
    w_gate = jnp.pad(w_in[:, :, c0:c1], ((0, 0), (0, 0), (0, LANES - fox_heads))).astype(BF16)
    b_gate = jnp.pad(b_forget, ((0, 0), (0, LANES - fox_heads)))[:, None, :]
    w_diff = w_in[:, :, c1:].astype(BF16)
    w_o_fox = w_o[:, :fox_w, :].astype(BF16)
    w_o_diff = w_o[:, fox_w:, :].astype(BF16)
    w_up_b = w_up.astype(BF16)
    w_down_b = w_down.astype(BF16)
    dqn = jnp.tile(diff_q_norm, (1, 2))[:, None, :]
    dkn = jnp.tile(diff_k_norm, (1, 2))[:, None, :]

    pt_flat = page_table.reshape(-1)
    cache_fk = cache_fox_k.reshape(depth, n_phys, page * fox_heads, fox_dim)
    cache_fv = cache_fox_v.reshape(depth, n_phys, page * fox_heads, fox_dim)
    cache_dv = cache_diff_v.reshape(depth, n_phys, page * diff_heads, diff_dim)
    cache_dk = jnp.transpose(cache_diff_k, (0, 1, 3, 4, 5, 2)).reshape(depth, n_phys, diff_w, page)
    cache_lf = jnp.transpose(cache_fox_logf, (0, 1, 3, 2))
    group = _largest_divisor(n_pages, PAGES_PER_STEP)

    fox_scale = fox_dim ** -0.5
    diff_scale = dk ** -0.5

    def trunk(x, pos, decode):
        m = x.shape[0]
        n_seq = dec_batch if decode else batch
        t = m // n_seq
        tm = m if decode else ROW_TILE
        cos, sin = _rope_tables(pos, dk)
        if decode:
            cos, sin = jnp.tile(cos, (n_seq, 1)), jnp.tile(sin, (n_seq, 1))
        q_dtype = F32 if decode else BF16
        q_unit = 1.0 if decode else LOG2E
        news = []
        for l in range(depth):
            lam_init = 0.8 - 0.6 * math.exp(-0.3 * l)
            lam_vecs = (lambda_q1[l][None], lambda_k1[l][None], lambda_q2[l][None],
                        lambda_k2[l][None])
            fox = _proj("fox", x, norm_attn[l][None], w_fox[l], fox_q_norm[l][None],
                        fox_k_norm[l][None], (w_gate[l], b_gate[l]), tm=tm, scale=fox_scale * q_unit,
                        q_dtype=q_dtype, emit_vt=not decode)
            dif = _proj("diff", x, norm_attn[l][None], w_diff[l], dqn[l], dkn[l], (cos, sin),
                        tm=tm, scale=diff_scale * q_unit, q_dtype=q_dtype, emit_vt=not decode)
            fq, fk, fkb, fv = fox[:4]
            dq, dkk, dkb, dv = dif[:4]
            logf = fox[-1]
            logf_h = logf[:, :fox_heads]
            if decode:
                pad = lambda a: jnp.pad(a.reshape(n_seq, t, -1), ((0, 0), (0, page - t), (0, 0)))
                rsum = _pastsum(cache_lf, pt_flat, l, n_seq)
                lw = jnp.repeat(logf_h.reshape(n_seq, t, fox_heads).transpose(0, 2, 1), t, axis=1)
                fox_o = _decode(fq, cache_fk, cache_fv, pad(fk), pad(fv), pt_flat, l, n_seq,
                                (rsum, lw), diff=False, group=group)
                diff_o = _decode(dq, cache_dk, cache_dv, pad(dkk), pad(dv), pt_flat, l, n_seq,
                                 lam_vecs + (diff_subln[l][None],), diff=True, group=group,
                                 lam_init=lam_init)
            else:
                qa, ka = _cumaug(logf, n_seq, fox_heads, tc=ROW_TILE)
                fox_o = _flash(fq, fkb, fox[4], n_seq, (qa, ka), diff=False, tq=ROW_TILE)
                diff_o = _flash(dq, dkb, dif[4], n_seq, lam_vecs + (diff_subln[l][:, None],),
                                diff=True, tq=ROW_TILE, lam_init=lam_init)
            x = _oproj(x, fox_o, diff_o, w_o_fox[l], w_o_diff[l], tm=tm)
            x = _mlp(x, norm_mlp[l][None], w_up_b[l], w_down_b[l], tm=tm, tf=MLP_FF_TILE)
            news.append((fk, fv, logf_h, dkk, dv))
        stack = lambda i, shape: jnp.stack([n[i] for n in news]).reshape((depth, n_seq, t) + shape)
        return (x.reshape(n_seq, t, d_model),
                stack(0, (fox_heads, fox_dim)), stack(1, (fox_heads, fox_dim)),
                stack(2, (fox_heads,)), stack(3, (diff_heads, 2, dk)),
                stack(4, (diff_heads, diff_dim)))

    pos_prompt = jnp.arange(seq, dtype=jnp.int32)
    pos_sample = past_len + jnp.arange(dec_seq, dtype=jnp.int32)
    yp = trunk(x_prompt.reshape(batch * seq, d_model), pos_prompt, False)
    ys = trunk(x_sample.reshape(dec_batch * dec_seq, d_model), pos_sample, True)
    return (yp[0], ys[0]) + yp[1:] + ys[1:]
```

```python
import functools
import math

import jax
import jax.numpy as jnp
from jax import lax
from jax.experimental import pallas as pl
from jax.experimental.pallas import tpu as pltpu

F32 = jnp.float32
BF16 = jnp.bfloat16
EPS = 1e-6
ROPE_THETA = 10000.0
NEG_INF = float("-inf")
MIB = 1024 * 1024

LANES = 128
MXU_WIDTH = 256
ROW_TILE = 512
MLP_FF_TILE = 512
PAGES_PER_STEP = 8
SUM_ROWS = 16
LOG2E = math.log2(math.e)


def _params(sem, vmem_mib):
    return pltpu.CompilerParams(dimension_semantics=sem, vmem_limit_bytes=vmem_mib * MIB)


def _split3(x):
    hi = x.astype(BF16)
    r1 = x - hi.astype(F32)
    mid = r1.astype(BF16)
    lo = (r1 - mid.astype(F32)).astype(BF16)
    return hi, mid, lo


def _dot(a, b):
    return jnp.dot(a, b, preferred_element_type=F32)


def _dot_nt(a, b):
    return lax.dot_general(a, b, (((1,), (1,)), ((), ())), preferred_element_type=F32)


def _dot3(x, w):
    hi, mid, lo = _split3(x)
    return _dot(hi, w) + _dot(mid, w) + _dot(lo, w)


def _dot3_left(w, x):
    hi, mid, lo = _split3(x)
    return _dot(w, hi) + _dot(w, mid) + _dot(w, lo)


def _log_sigmoid(x):
    return jnp.minimum(x, 0.0) - jnp.log1p(jnp.exp(-jnp.abs(x)))


def _rms_rows(x, g):
    ms = jnp.mean(x * x, axis=-1, keepdims=True)
    return x * lax.rsqrt(ms + EPS) * g


def _proj_kernel(*refs, kind, n_heads, scale, final, n_prev):
    x_ref, g_ref, w_ref, qn_ref, kn_ref, e0_ref, e1_ref = refs[:7]
    outs = refs[7 + n_prev:]
    fox = kind == "fox"
    if final:
        q_out, kb_out, vt_out = outs[:3]
        if fox:
            logf_out, k_fin, v_fin, logft_out = outs[3:]
        else:
            k_fin, v_fin = outs[3:]
    else:
        q_out, k_out, v_out = outs[:3]
        if fox:
            logf_out = outs[3]
    tm = x_ref.shape[0]
    w = n_heads * LANES

    xn = _rms_rows(x_ref[...], g_ref[...]).astype(BF16)
    if fox:
        logf = _log_sigmoid(_dot(xn, e0_ref[...]) + e1_ref[...])
        logf_out[...] = logf
        if final:
            logft_out[...] = logf.T[0:n_heads, :]

    def heads(segment):
        acc = _dot(xn, w_ref[:, segment * w:(segment + 1) * w])
        return [acc[:, h * LANES:(h + 1) * LANES] for h in range(n_heads)]

    if fox:
        def normed(blk, gain):
            return _rms_rows(blk, gain)
    else:
        lane = lax.broadcasted_iota(jnp.int32, (tm, LANES), 1)
        lane_lo = lane < (LANES // 2)
        rot_up = (lane % (LANES // 2)) < (LANES // 4)
        cos = e0_ref[...]
        sin = e1_ref[...]

        def normed(blk, gain):
            sq = blk * blk
            s_lo = jnp.sum(jnp.where(lane_lo, sq, 0.0), axis=-1, keepdims=True)
            s_hi = jnp.sum(jnp.where(lane_lo, 0.0, sq), axis=-1, keepdims=True)
            ms = jnp.where(lane_lo, s_lo, s_hi) * (2.0 / LANES)
            y = blk * lax.rsqrt(ms + EPS) * gain
            up = pltpu.roll(y, LANES - LANES // 4, axis=1)
            dn = pltpu.roll(y, LANES // 4, axis=1)
            return y * cos + jnp.where(rot_up, up, dn) * sin

    def head_lanes(h):
        return slice(h * LANES, (h + 1) * LANES)

    def token_head_rows(h):
        return pl.ds(h, tm, stride=n_heads)

    gain = qn_ref[...]
    for h, blk in enumerate(heads(0)):
        q_out[:, head_lanes(h)] = (normed(blk, gain) * scale).astype(q_out.dtype)

    gain = kn_ref[...]
    for h, blk in enumerate(heads(1)):
        y = normed(blk, gain)
        if not final:
            k_out[:, head_lanes(h)] = y
            continue
        kb_out[:, head_lanes(h)] = y.astype(BF16)
        if fox:
            k_fin[token_head_rows(h), :] = y
        else:
            k_fin[head_lanes(h), :] = y.T

    acc = _dot(xn, w_ref[:, 2 * w:])
    if final:
        vt_out[...] = acc.T.astype(BF16).reshape(vt_out.shape)
        for h in range(n_heads):
            v_fin[token_head_rows(h), :] = acc[:, head_lanes(h)]
    else:
        v_out[...] = acc


def _proj(kind, x, g, w3, qn, kn, extra, *, tm, scale, final, batch=1, depth=1, layer=0,
          prev=()):
    m, d = x.shape
    w = w3.shape[1] // 3
    n_heads = w // LANES
    fox = kind == "fox"
    nb = m // batch // tm
    row = lambda i: (i, 0)
    const = lambda i: (0, 0)
    in_specs = [
        pl.BlockSpec((tm, d), row),
        pl.BlockSpec((1, d), const),
        pl.BlockSpec((d, 3 * w), const, pipeline_mode=pl.Buffered(1)),
        pl.BlockSpec((1, LANES), const),
        pl.BlockSpec((1, LANES), const),
    ]
    if fox:
        in_specs += [pl.BlockSpec((d, LANES), const), pl.BlockSpec((1, LANES), const)]
    else:
        tab = lambda i: (i % nb, 0)
        in_specs += [pl.BlockSpec((tm, LANES), tab), pl.BlockSpec((tm, LANES), tab)]
    in_specs += [pl.BlockSpec(memory_space=pl.ANY) for _ in prev]

    rows = (jax.ShapeDtypeStruct((m, w), F32), pl.BlockSpec((tm, w), row))
    rows_bf16 = (jax.ShapeDtypeStruct((m, w), BF16), pl.BlockSpec((tm, w), row))
    logf = (jax.ShapeDtypeStruct((m, LANES), F32), pl.BlockSpec((tm, LANES), row))
    if final:
        seq = m // batch
        token_head = (jax.ShapeDtypeStruct((depth, m * n_heads, LANES), F32),
                      pl.BlockSpec((None, tm * n_heads, LANES), lambda i: (layer, i, 0)))
        token_minor = lambda r: (jax.ShapeDtypeStruct((depth, batch, r, seq), F32),
                                 pl.BlockSpec((None, None, r, tm),
                                              lambda i: (layer, i // nb, 0, i % nb)))
        vt = (jax.ShapeDtypeStruct((n_heads, m // tm, LANES, tm), BF16),
              pl.BlockSpec((n_heads, 1, LANES, tm), lambda i: (0, i, 0, 0)))
        outs = [rows_bf16, rows_bf16, vt]
        if fox:
            outs += [logf, token_head, token_head, token_minor(n_heads)]
        else:
            outs += [token_minor(w), token_head]
        first_stacked = len(outs) - (3 if fox else 2)
        aliases = {7 + a: first_stacked + a for a in range(len(prev))}
    else:
        outs = [rows, rows, rows] + ([logf] if fox else [])
        aliases = {}
    return pl.pallas_call(
        functools.partial(_proj_kernel, kind=kind, n_heads=n_heads, scale=scale, final=final,
                          n_prev=len(prev)),
        grid=(m // tm,),
        in_specs=in_specs,
        out_specs=[o[1] for o in outs],
        out_shape=[o[0] for o in outs],
        input_output_aliases=aliases,
        compiler_params=_params(("parallel",), 60),
    )(x, g, w3, qn, kn, *extra, *prev)


def _cumaug_kernel(logf_ref, tri_ref, selq_ref, selk_ref, oneq_ref, onek_ref,
                   qa_out, ka_out, carry_sc):
    @pl.when(pl.program_id(1) == 0)
    def _():
        carry_sc[...] = jnp.zeros_like(carry_sc)

    cum = _dot3_left(tri_ref[...], logf_ref[...]) + carry_sc[0:1, :]
    carry_sc[0:1, :] = cum[cum.shape[0] - 1:, :]
    pieces = jnp.concatenate(_split3(cum * LOG2E), axis=1)
    qa_out[...] = (_dot(pieces, selq_ref[...]) + oneq_ref[...]).astype(BF16)
    ka_out[...] = (_dot(pieces, selk_ref[...]) + onek_ref[...]).astype(BF16)


def _aug_constants(n_heads):
    w = n_heads * LANES
    r = jnp.arange(3 * LANES)
    c = jnp.arange(w)
    piece, head = r // LANES, r % LANES
    col_head, col_j = c // LANES, c % LANES
    same = (head[:, None] == col_head[None, :]) & (head[:, None] < n_heads)
    selq = jnp.where(same & (col_j[None, :] == piece[:, None]), 1.0, 0.0).astype(BF16)
    selk = jnp.where(same & (col_j[None, :] == piece[:, None] + 3), -1.0, 0.0).astype(BF16)
    oneq = jnp.where((col_j >= 3) & (col_j < 6), 1.0, 0.0).astype(F32)[None, :]
    onek = jnp.where(col_j < 3, 1.0, 0.0).astype(F32)[None, :]
    return selq, selk, oneq, onek


def _cumaug(logf, batch, n_heads, *, tc):
    m = logf.shape[0]
    t = m // batch
    nt = t // tc
    w = n_heads * LANES
    tri = (jnp.arange(tc)[:, None] >= jnp.arange(tc)[None, :]).astype(BF16)
    selq, selk, oneq, onek = _aug_constants(n_heads)
    const = lambda b, i: (0, 0)
    row = lambda b, i: (b * nt + i, 0)
    return pl.pallas_call(
        _cumaug_kernel,
        grid=(batch, nt),
        in_specs=[
            pl.BlockSpec((tc, LANES), row),
            pl.BlockSpec((tc, tc), const),
            pl.BlockSpec((3 * LANES, w), const),
            pl.BlockSpec((3 * LANES, w), const),
            pl.BlockSpec((1, w), const),
            pl.BlockSpec((1, w), const),
        ],
        out_specs=[pl.BlockSpec((tc, w), row), pl.BlockSpec((tc, w), row)],
        out_shape=[jax.ShapeDtypeStruct((m, w), BF16), jax.ShapeDtypeStruct((m, w), BF16)],
        scratch_shapes=[pltpu.VMEM((8, LANES), F32)],
        compiler_params=_params(("parallel", "arbitrary"), 32),
    )(logf, tri, selq, selk, oneq, onek)


def _lambda(lq1, lk1, lq2, lk2, lam_init):
    a = jnp.sum(lq1[...] * lk1[...], axis=-1, keepdims=True)
    b = jnp.sum(lq2[...] * lk2[...], axis=-1, keepdims=True)
    return jnp.exp(a) - jnp.exp(b) + lam_init


def _flash_kernel(*refs, diff, tq, lam_init):
    if diff:
        (q_ref, k_ref, vt_ref, lq1, lk1, lq2, lk2, sub_ref,
         o_ref, q_sc, m_sc, acc_sc, sa_sc, sb_sc) = refs
    else:
        (q_ref, qa_ref, k_ref, ka_ref, vt_ref,
         o_ref, q_sc, m_sc, acc_sc, sa_sc, sb_sc) = refs
    i = pl.program_id(2)
    cols = q_sc.shape[0]

    if diff:
        q = q_ref[...]
        lane = lax.broadcasted_iota(jnp.int32, q.shape, 1)
        zero = jnp.zeros_like(q)
        q_sc[0:tq, :] = jnp.where(lane < LANES // 2, q, zero)
        q_sc[tq:, :] = jnp.where(lane < LANES // 2, zero, q)
    else:
        q_sc[:, 0:LANES] = q_ref[...]
        q_sc[:, LANES:] = qa_ref[...]
    m_sc[...] = jnp.full_like(m_sc, NEG_INF)
    acc_sc[...] = jnp.zeros_like(acc_sc)
    ones = jnp.ones((SUM_ROWS, tq), BF16)

    chunks = [slice(c * MXU_WIDTH, (c + 1) * MXU_WIDTH) for c in range(cols // MXU_WIDTH)]

    def scores(j, s_sc):
        start = pl.multiple_of(j * tq, tq)
        k = k_ref[pl.ds(start, tq), :]
        if not diff:
            k = jnp.concatenate([k, ka_ref[pl.ds(start, tq), :]], axis=1)
        for cs in chunks:
            s_sc[:, cs] = _dot_nt(k, q_sc[cs, :])

    def accumulate(j, s_sc, masked):
        vt = jnp.concatenate([vt_ref[j], ones], axis=0)
        for c, cs in enumerate(chunks):
            s = s_sc[:, cs]
            if masked:
                key = lax.broadcasted_iota(jnp.int32, s.shape, 0)
                qry = (lax.broadcasted_iota(jnp.int32, s.shape, 1) + c * MXU_WIDTH) % tq
                s = jnp.where(key <= qry, s, NEG_INF)
            m_prev = m_sc[:, cs]
            m_new = jnp.maximum(m_prev, jnp.max(s, axis=0, keepdims=True))
            alpha = jnp.exp2(m_prev - m_new)
            p = jnp.exp2(s - m_new).astype(BF16)
            acc_sc[:, cs] = alpha * acc_sc[:, cs] + _dot(vt, p)
            m_sc[:, cs] = m_new

    scores(0, sa_sc)

    def pair(jj, carry):
        j = 2 * jj
        scores(j + 1, sb_sc)
        accumulate(j, sa_sc, False)
        scores(j + 2, sa_sc)
        accumulate(j + 1, sb_sc, False)
        return carry

    lax.fori_loop(0, i // 2, pair, 0)

    @pl.when(i % 2 == 0)
    def _():
        accumulate(i, sa_sc, True)

    @pl.when(i % 2 == 1)
    def _():
        scores(i, sb_sc)
        accumulate(i - 1, sa_sc, False)
        accumulate(i, sb_sc, True)

    out_t = acc_sc[0:LANES, :] / acc_sc[LANES:LANES + 1, :]
    if diff:
        lam = _lambda(lq1, lk1, lq2, lk2, lam_init)
        o = out_t[:, 0:tq] - lam * out_t[:, tq:]
        ms = jnp.mean(o * o, axis=0, keepdims=True)
        o = o * lax.rsqrt(ms + EPS) * sub_ref[...] * (1.0 - lam_init)
        o_ref[...] = o.T.astype(o_ref.dtype)
    else:
        o_ref[...] = out_t.T.astype(o_ref.dtype)


def _flash(q, k, vt, batch, extra, *, diff, tq, lam_init=0.0):
    m, w = q.shape
    n_heads = w // LANES
    t = m // batch
    nq = t // tq
    qspec = pl.BlockSpec((tq, LANES), lambda b, h, i: (b * nq + i, h))
    kspec = pl.BlockSpec((t, LANES), lambda b, h, i: (b, h))
    vspec = pl.BlockSpec((None, nq, LANES, tq), lambda b, h, i: (h, b, 0, 0))
    if diff:
        vec = pl.BlockSpec((1, LANES // 2), lambda b, h, i: (0, 0))
        sub = pl.BlockSpec((LANES, 1), lambda b, h, i: (0, 0))
        in_specs = [qspec, kspec, vspec, vec, vec, vec, vec, sub]
        args = (q, k, vt) + tuple(extra)
        cols, qw = 2 * tq, LANES
    else:
        qa, ka = extra
        in_specs = [qspec, qspec, kspec, kspec, vspec]
        args = (q, qa, k, ka, vt)
        cols, qw = tq, 2 * LANES
    return pl.pallas_call(
        functools.partial(_flash_kernel, diff=diff, tq=tq, lam_init=lam_init),
        grid=(batch, n_heads, nq),
        in_specs=in_specs,
        out_specs=qspec,
        out_shape=jax.ShapeDtypeStruct((m, w), BF16),
        scratch_shapes=[
            pltpu.VMEM((cols, qw), BF16),
            pltpu.VMEM((1, cols), F32),
            pltpu.VMEM((LANES + SUM_ROWS, cols), F32),
            pltpu.VMEM((tq, cols), F32),
            pltpu.VMEM((tq, cols), F32),
        ],
        compiler_params=_params(("parallel", "parallel", "arbitrary"), 48),
    )(*args)


def _pastsum_kernel(pt_ref, logf_hbm, u_ref, ubig_ref, r_out, buf, sem, *, layer, n_pages):
    b = pl.program_id(0)

    def copy(p):
        page = pt_ref[b * n_pages + p]
        return pltpu.make_async_copy(logf_hbm.at[layer, page], buf.at[p], sem)

    def start(p, c):
        copy(p).start()
        return c

    def wait(p, c):
        copy(p).wait()
        return c

    lax.fori_loop(0, n_pages, start, 0)
    lax.fori_loop(0, n_pages, wait, 0)

    xt = buf[...].reshape(r_out.shape)
    within = _dot3(xt, u_ref[...])
    later = jnp.sum(_dot3_left(ubig_ref[...], xt), axis=1, keepdims=True)
    r_out[...] = within + later


def _pastsum(cache_logf_t, page_table_flat, layer, dec_batch):
    n_pages = page_table_flat.shape[0] // dec_batch
    n_heads, t = cache_logf_t.shape[2:]
    u = (jnp.arange(t)[:, None] > jnp.arange(t)[None, :]).astype(BF16)
    r = jnp.arange(n_pages * n_heads)
    ubig = ((r[None, :] % n_heads == r[:, None] % n_heads)
            & (r[None, :] // n_heads > r[:, None] // n_heads)).astype(BF16)
    rows = n_pages * n_heads
    const = lambda b, pt: (0, 0)
    grid_spec = pltpu.PrefetchScalarGridSpec(
        num_scalar_prefetch=1,
        grid=(dec_batch,),
        in_specs=[
            pl.BlockSpec(memory_space=pl.ANY),
            pl.BlockSpec((t, t), const),
            pl.BlockSpec((rows, rows), const),
        ],
        out_specs=pl.BlockSpec((None, rows, t), lambda b, pt: (b, 0, 0)),
        scratch_shapes=[pltpu.VMEM((n_pages, n_heads, t), F32), pltpu.SemaphoreType.DMA(())],
    )
    return pl.pallas_call(
        functools.partial(_pastsum_kernel, layer=layer, n_pages=n_pages),
        grid_spec=grid_spec,
        out_shape=jax.ShapeDtypeStruct((dec_batch, rows, t), F32),
        compiler_params=_params(("arbitrary",), 32),
    )(page_table_flat, cache_logf_t, u, ubig)


def _decode_kernel(*refs, diff, n_steps, group, n_heads, t_new, page, lam_init):
    refs = list(refs)
    pt_ref, q_ref = refs[0], refs[1]
    k_refs = refs[2:2 + group]
    v_refs = refs[2 + group:2 + 2 * group]
    rest = refs[2 + 2 * group:]
    if diff:
        (kn_ref, vn_ref, lq1, lk1, lq2, lk2, sub_ref,
         o_ref, q_sc, m_sc, l_sc, acc_sc) = rest
    else:
        (kn_ref, vn_ref, r_ref, lw_ref,
         o_ref, q_sc, m_sc, l_sc, acc_sc, qc_sc) = rest
    p = pl.program_id(1)
    rows = q_sc.shape[0]
    groups = 2 * n_heads if diff else n_heads
    group_w = q_sc.shape[1] // groups

    def q_index(shape):
        return lax.broadcasted_iota(jnp.int32, shape, 0) % t_new

    def page_rows(ref):
        heads = [ref[pl.ds(h, page, stride=n_heads), :] for h in range(n_heads)]
        return jnp.concatenate(heads, axis=1).astype(BF16)

    @pl.when(p == 0)
    def _():
        q = jnp.concatenate([q_ref[...]] * (rows // t_new), axis=0)
        r = lax.broadcasted_iota(jnp.int32, q.shape, 0)
        lane = lax.broadcasted_iota(jnp.int32, q.shape, 1)
        if diff:
            grp = ((r // t_new) % n_heads) * 2 + r // (t_new * n_heads)
        else:
            grp = r // t_new
        q_sc[...] = jnp.where(lane // group_w == grp, q, 0.0).astype(BF16)
        m_sc[...] = jnp.full_like(m_sc, NEG_INF)
        l_sc[...] = jnp.zeros_like(l_sc)
        acc_sc[...] = jnp.zeros_like(acc_sc)
        if not diff:
            lw = lw_ref[...]
            i_idx = lax.broadcasted_iota(jnp.int32, lw.shape, 1)
            qc = jnp.sum(jnp.where(i_idx <= q_index(lw.shape), lw, 0.0), axis=1, keepdims=True)
            qc_sc[...] = qc

    def update(s, values):
        m_prev = m_sc[...]
        m_new = jnp.maximum(m_prev, jnp.max(s, axis=-1, keepdims=True))
        alpha = jnp.exp(m_prev - m_new)
        pr = jnp.exp(s - m_new)
        l_sc[...] = alpha * l_sc[...] + jnp.sum(pr, axis=-1, keepdims=True)
        pr = pr.astype(BF16)
        pv = _dot(pr[:, 0:page], values[0])
        for g in range(1, len(values)):
            pv = pv + _dot(pr[:, g * page:(g + 1) * page], values[g])
        acc_sc[...] = alpha * acc_sc[...] + pv
        m_sc[...] = m_new

    @pl.when(p < n_steps)
    def _():
        parts = []
        for g in range(group):
            if diff:
                s = _dot(q_sc[...], k_refs[g][...].astype(BF16))
            else:
                s = _dot_nt(q_sc[...], page_rows(k_refs[g]))
                r8 = r_ref[g * n_heads:(g + 1) * n_heads, :]
                bias = jnp.concatenate(
                    [jnp.broadcast_to(r8[h:h + 1, :], (t_new, page)) for h in range(n_heads)],
                    axis=0)
                s = s + (qc_sc[...] + bias)
            parts.append(s)
        update(jnp.concatenate(parts, axis=1), [page_rows(v) for v in v_refs])

    @pl.when(p == n_steps)
    def _():
        s = _dot_nt(q_sc[...], kn_ref[...].astype(BF16))
        col = lax.broadcasted_iota(jnp.int32, s.shape, 1)
        qi = q_index(s.shape)
        if not diff:
            lw = lw_ref[...]
            bias = jnp.zeros(s.shape, F32)
            for i in range(t_new):
                bias = bias + jnp.where((qi >= i) & (col < i), lw[:, i:i + 1], 0.0)
            s = s + bias
        s = jnp.where(col <= qi, s, NEG_INF)
        update(s, [vn_ref[...].astype(BF16)])

        acc = acc_sc[...]
        inv_l = 1.0 / l_sc[...]
        if diff:
            lam = _lambda(lq1, lk1, lq2, lk2, lam_init)
            half = n_heads * t_new
        for h in range(n_heads):
            lanes = slice(h * LANES, (h + 1) * LANES)
            r0 = h * t_new
            o = acc[r0:r0 + t_new, lanes] * inv_l[r0:r0 + t_new, :]
            if diff:
                o2 = acc[half + r0:half + r0 + t_new, lanes] * inv_l[half + r0:half + r0 + t_new, :]
                o = _rms_rows(o - lam * o2, sub_ref[...]) * (1.0 - lam_init)
            o_ref[:, lanes] = o


def _decode(q, cache_k, cache_v, k_new, v_new, page_table_flat, layer, dec_batch, extra,
            *, diff, group, lam_init=0.0):
    w = q.shape[1]
    t_new = q.shape[0] // dec_batch
    page = k_new.shape[1]
    n_pages = page_table_flat.shape[0] // dec_batch
    n_steps = n_pages // group
    n_heads = w // LANES
    rows = (2 if diff else 1) * n_heads * t_new

    def page_map(g):
        def index(b, p, pt):
            return (layer, pt[b * n_pages + jnp.minimum(p, n_steps - 1) * group + g], 0, 0)
        return index

    per_b = lambda b, p, pt: (b, 0)
    per_b3 = lambda b, p, pt: (b, 0, 0)
    kblock = (None, None) + cache_k.shape[2:]
    vblock = (None, None) + cache_v.shape[2:]
    in_specs = [pl.BlockSpec((t_new, w), per_b)]
    in_specs += [pl.BlockSpec(kblock, page_map(g)) for g in range(group)]
    in_specs += [pl.BlockSpec(vblock, page_map(g)) for g in range(group)]
    in_specs += [pl.BlockSpec((None, page, w), per_b3), pl.BlockSpec((None, page, w), per_b3)]
    scratch = [
        pltpu.VMEM((rows, w), BF16),
        pltpu.VMEM((rows, 1), F32),
        pltpu.VMEM((rows, 1), F32),
        pltpu.VMEM((rows, w), F32),
    ]
    if diff:
        vec = pl.BlockSpec((1, LANES // 2), lambda b, p, pt: (0, 0))
        in_specs += [vec, vec, vec, vec, pl.BlockSpec((1, LANES), lambda b, p, pt: (0, 0))]
    else:
        in_specs += [
            pl.BlockSpec((None, group * n_heads, page),
                         lambda b, p, pt: (b, jnp.minimum(p, n_steps - 1), 0)),
            pl.BlockSpec((None, rows, t_new), per_b3),
        ]
        scratch.append(pltpu.VMEM((rows, 1), F32))
    grid_spec = pltpu.PrefetchScalarGridSpec(
        num_scalar_prefetch=1,
        grid=(dec_batch, n_steps + 1),
        in_specs=in_specs,
        out_specs=pl.BlockSpec((t_new, w), per_b),
        scratch_shapes=scratch,
    )
    return pl.pallas_call(
        functools.partial(_decode_kernel, diff=diff, n_steps=n_steps, group=group,
                          n_heads=n_heads, t_new=t_new, page=page, lam_init=lam_init),
        grid_spec=grid_spec,
        out_shape=jax.ShapeDtypeStruct((dec_batch * t_new, w), F32),
        compiler_params=_params(("parallel", "arbitrary"), 48),
    )(page_table_flat, q, *([cache_k] * group), *([cache_v] * group), k_new, v_new, *extra)


def _oproj_kernel(x_ref, a_ref, b_ref, wa_ref, wb_ref, o_ref):
    o_ref[...] = (x_ref[...] + _dot(a_ref[...].astype(BF16), wa_ref[...])
                  + _dot(b_ref[...].astype(BF16), wb_ref[...]))


def _oproj(x, a, b, wa, wb, *, tm):
    m, d = x.shape
    w = a.shape[1]
    row = lambda i: (i, 0)
    const = lambda i: (0, 0)
    return pl.pallas_call(
        _oproj_kernel,
        grid=(m // tm,),
        in_specs=[
            pl.BlockSpec((tm, d), row),
            pl.BlockSpec((tm, w), row),
            pl.BlockSpec((tm, w), row),
            pl.BlockSpec((w, d), const),
            pl.BlockSpec((w, d), const),
        ],
        out_specs=pl.BlockSpec((tm, d), row),
        out_shape=jax.ShapeDtypeStruct((m, d), F32),
        compiler_params=_params(("parallel",), 48),
    )(x, a, b, wa, wb)


def _mlp_kernel(x_ref, g_ref, wu_ref, wd_ref, o_ref, xn_sc, acc_sc):
    j = pl.program_id(1)

    @pl.when(j == 0)
    def _():
        x = x_ref[...]
        xn_sc[...] = _rms_rows(x, g_ref[...]).astype(BF16)
        acc_sc[...] = x

    h = jnp.maximum(_dot(xn_sc[...], wu_ref[...]), 0.0)
    acc_sc[...] += _dot((h * h).astype(BF16), wd_ref[...])

    @pl.when(j == pl.num_programs(1) - 1)
    def _():
        o_ref[...] = acc_sc[...]


def _mlp(x, g, wu, wd, *, tm, tf):
    m, d = x.shape
    f = wu.shape[1]
    row = lambda i, j: (i, 0)
    return pl.pallas_call(
        _mlp_kernel,
        grid=(m // tm, f // tf),
        in_specs=[
            pl.BlockSpec((tm, d), row),
            pl.BlockSpec((1, d), lambda i, j: (0, 0)),
            pl.BlockSpec((d, tf), lambda i, j: (0, j)),
            pl.BlockSpec((tf, d), lambda i, j: (j, 0)),
        ],
        out_specs=pl.BlockSpec((tm, d), row),
        out_shape=jax.ShapeDtypeStruct((m, d), F32),
        scratch_shapes=[pltpu.VMEM((tm, d), BF16), pltpu.VMEM((tm, d), F32)],
        compiler_params=_params(("parallel", "arbitrary"), 48),
    )(x, g, wu, wd)


def _rope_tables(pos, dk):
    inv_freq = ROPE_THETA ** (-jnp.arange(0, dk, 2, dtype=F32) / dk)
    ang = pos.astype(F32)[:, None] * inv_freq[None, :]
    cos, sin = jnp.cos(ang), jnp.sin(ang)
    reps = LANES // dk
    cos = jnp.tile(jnp.concatenate([cos, cos], axis=-1), (1, reps))
    sin = jnp.tile(jnp.concatenate([-sin, sin], axis=-1), (1, reps))
    return cos, sin


def _largest_divisor(n, cap):
    return max(d for d in range(1, cap + 1) if n % d == 0)


def kernel(x_prompt, x_sample, cache_fox_k, cache_fox_v, cache_fox_logf, cache_diff_k, cache_diff_v, page_table, norm_attn, w_in, b_forget, fox_q_norm, fox_k_norm, diff_q_norm, diff_k_norm, lambda_q1, lambda_k1, lambda_q2, lambda_k2, diff_subln, w_o, norm_mlp, w_up, w_down):
    depth = w_in.shape[0]
    batch, seq, d_model = x_prompt.shape
    dec_batch, dec_seq, _ = x_sample.shape
    _, n_phys, page, fox_heads, fox_dim = cache_fox_k.shape
    diff_heads, diff_dim = cache_diff_v.shape[3], cache_diff_v.shape[4]
    dk = cache_diff_k.shape[5]
    fox_w = fox_heads * fox_dim
    diff_w = diff_heads * diff_dim
    n_pages = page_table.shape[1]
    past_len = n_pages * page
    assert fox_dim == LANES and diff_dim == LANES and 2 * dk == LANES
    assert seq % ROW_TILE == 0

    c0 = 3 * fox_w
    c1 = c0 + fox_heads
    w_fox = w_in[:, :, :c0].astype(BF16)
    w_gate = jnp.pad(w_in[:, :, c0:c1], ((0, 0), (0, 0), (0, LANES - fox_heads))).astype(BF16)
    b_gate = jnp.pad(b_forget, ((0, 0), (0, LANES - fox_heads)))[:, None, :]
    w_diff = w_in[:, :, c1:].astype(BF16)
    w_o_fox = w_o[:, :fox_w, :].astype(BF16)
    w_o_diff = w_o[:, fox_w:, :].astype(BF16)
    w_up_b = w_up.astype(BF16)
    w_down_b = w_down.astype(BF16)
    dqn = jnp.tile(diff_q_norm, (1, 2))[:, None, :]
    dkn = jnp.tile(diff_k_norm, (1, 2))[:, None, :]

    pt_flat = page_table.reshape(-1)
    cache_fk = cache_fox_k.reshape(depth, n_phys, page * fox_heads, fox_dim)
    cache_fv = cache_fox_v.reshape(depth, n_phys, page * fox_heads, fox_dim)
    cache_dv = cache_diff_v.reshape(depth, n_phys, page * diff_heads, diff_dim)
    cache_dk = jnp.transpose(cache_diff_k, (0, 1, 3, 4, 5, 2)).reshape(depth, n_phys, diff_w, page)
    cache_lf = jnp.transpose(cache_fox_logf, (0, 1, 3, 2))
    group = _largest_divisor(n_pages, PAGES_PER_STEP)

    fox_scale = fox_dim ** -0.5
    diff_scale = dk ** -0.5

    def trunk(x, pos, decode):
        m = x.shape[0]
        n_seq = dec_batch if decode else batch
        t = m // n_seq
        tm = m if decode else ROW_TILE
        cos, sin = _rope_tables(pos, dk)
        if decode:
            cos, sin = jnp.tile(cos, (n_seq, 1)), jnp.tile(sin, (n_seq, 1))
        q_unit = 1.0 if decode else LOG2E
        news = []
        fox_stacked, diff_stacked = (), ()
        if not decode:
            token_head = jnp.zeros((depth, m * fox_heads, fox_dim), F32)
            fox_stacked = (token_head, token_head, jnp.zeros((depth, n_seq, fox_heads, t), F32))
            diff_stacked = (jnp.zeros((depth, n_seq, diff_w, t), F32), token_head)
        for l in range(depth):
            lam_init = 0.8 - 0.6 * math.exp(-0.3 * l)
            lam_vecs = (lambda_q1[l][None], lambda_k1[l][None], lambda_q2[l][None],
                        lambda_k2[l][None])
            place = dict(final=not decode, batch=n_seq, depth=depth, layer=l)
            fox = _proj("fox", x, norm_attn[l][None], w_fox[l], fox_q_norm[l][None],
                        fox_k_norm[l][None], (w_gate[l], b_gate[l]), tm=tm,
                        scale=fox_scale * q_unit, prev=fox_stacked, **place)
            dif = _proj("diff", x, norm_attn[l][None], w_diff[l], dqn[l], dkn[l], (cos, sin),
                        tm=tm, scale=diff_scale * q_unit, prev=diff_stacked, **place)
            logf = fox[3]
            if decode:
                fq, fk, fv = fox[:3]
                dq, dkk, dv = dif[:3]
                logf_h = logf[:, :fox_heads]
                pad = lambda a: jnp.pad(a.reshape(n_seq, t, -1), ((0, 0), (0, page - t), (0, 0)))
                rsum = _pastsum(cache_lf, pt_flat, l, n_seq)
                lw = jnp.repeat(logf_h.reshape(n_seq, t, fox_heads).transpose(0, 2, 1), t, axis=1)
                fox_o = _decode(fq, cache_fk, cache_fv, pad(fk), pad(fv), pt_flat, l, n_seq,
                                (rsum, lw), diff=False, group=group)
                diff_o = _decode(dq, cache_dk, cache_dv, pad(dkk), pad(dv), pt_flat, l, n_seq,
                                 lam_vecs + (diff_subln[l][None],), diff=True, group=group,
                                 lam_init=lam_init)
                news.append((fk, fv, logf_h, dkk, dv))
            else:
                fq, fkb, fvt = fox[:3]
                dq, dkb, dvt = dif[:3]
                fox_stacked, diff_stacked = tuple(fox[4:]), tuple(dif[3:])
                qa, ka = _cumaug(logf, n_seq, fox_heads, tc=ROW_TILE)
                fox_o = _flash(fq, fkb, fvt, n_seq, (qa, ka), diff=False, tq=ROW_TILE)
                diff_o = _flash(dq, dkb, dvt, n_seq, lam_vecs + (diff_subln[l][:, None],),
                                diff=True, tq=ROW_TILE, lam_init=lam_init)
            x = _oproj(x, fox_o, diff_o, w_o_fox[l], w_o_diff[l], tm=tm)
            x = _mlp(x, norm_mlp[l][None], w_up_b[l], w_down_b[l], tm=tm, tf=MLP_FF_TILE)
        lead = (depth, n_seq, t)
        if decode:
            stack = lambda i, shape: jnp.stack([n[i] for n in news]).reshape(lead + shape)
            new_cache = (stack(0, (fox_heads, fox_dim)), stack(1, (fox_heads, fox_dim)),
                         stack(2, (fox_heads,)), stack(3, (diff_heads, 2, dk)),
                         stack(4, (diff_heads, diff_dim)))
        else:
            fk, fv, logf_t = fox_stacked
            dk_t, dv = diff_stacked
            new_cache = (fk.reshape(lead + (fox_heads, fox_dim)),
                         fv.reshape(lead + (fox_heads, fox_dim)),
                         jnp.transpose(logf_t, (0, 1, 3, 2)),
                         jnp.transpose(dk_t.reshape(depth, n_seq, diff_heads, 2, dk, t),
                                       (0, 1, 5, 2, 3, 4)),
                         dv.reshape(lead + (diff_heads, diff_dim)))
        return (x.reshape(n_seq, t, d_model),) + new_cache

    pos_prompt = jnp.arange(seq, dtype=jnp.int32)
    pos_sample = past_len + jnp.arange(dec_seq, dtype=jnp.int32)
    yp = trunk(x_prompt.reshape(batch * seq, d_model), pos_prompt, False)
    ys = trunk(x_sample.reshape(dec_batch * dec_seq, d_model), pos_sample, True)
    return (yp[0], ys[0]) + yp[1:] + ys[1:]
```

```python
import functools
import math

import jax
import jax.numpy as jnp
from jax import lax
from jax.experimental import pallas as pl
from jax.experimental.pallas import tpu as pltpu

F32 = jnp.float32
BF16 = jnp.bfloat16
EPS = 1e-6
ROPE_THETA = 10000.0
NEG_INF = float("-inf")
MIB = 1024 * 1024

LANES = 128
MXU_WIDTH = 256
ROW_TILE = 512
MLP_FF_TILE = 1024
PAGES_PER_STEP = 16
SUM_ROWS = 16
LOG2E = math.log2(math.e)


def _params(sem, vmem_mib):
    return pltpu.CompilerParams(dimension_semantics=sem, vmem_limit_bytes=vmem_mib * MIB)


def _split3(x):
    hi = x.astype(BF16)
    r1 = x - hi.astype(F32)
    mid = r1.astype(BF16)
    lo = (r1 - mid.astype(F32)).astype(BF16)
    return hi, mid, lo


def _dot(a, b):
    return jnp.dot(a, b, preferred_element_type=F32)


def _dot_nt(a, b):
    return lax.dot_general(a, b, (((1,), (1,)), ((), ())), preferred_element_type=F32)


def _dot3(x, w):
    hi, mid, lo = _split3(x)
    return _dot(hi, w) + _dot(mid, w) + _dot(lo, w)


def _dot3_left(w, x):
    hi, mid, lo = _split3(x)
    return _dot(w, hi) + _dot(w, mid) + _dot(w, lo)


def _log_sigmoid(x):
    return jnp.minimum(x, 0.0) - jnp.log1p(jnp.exp(-jnp.abs(x)))


def _rms_rows(x, g):
    ms = jnp.mean(x * x, axis=-1, keepdims=True)
    return x * lax.rsqrt(ms + EPS) * g


def _proj_kernel(*refs, kind, n_heads, scale, final, layer, n_prev):
    x_ref, g_ref, w_ref, qn_ref, kn_ref, e0_ref, e1_ref = refs[:7]
    outs = refs[7 + n_prev:]
    fox = kind == "fox"
    if final:
        q_out, kb_out, vt_out = outs[:3]
        stacked = list(outs[4:] if fox else outs[3:])
        if n_prev == 0:
            for i, ref in enumerate(stacked):
                for other in range(ref.shape[0]):
                    if other != layer:
                        ref[other] = jnp.zeros(ref.shape[1:], F32)
                stacked[i] = ref.at[layer]
        if fox:
            logf_out = outs[3]
            k_fin, v_fin, logft_out = stacked
        else:
            k_fin, v_fin = stacked
    else:
        q_out, k_out, v_out = outs[:3]
        if fox:
            logf_out = outs[3]
    tm = x_ref.shape[0]
    w = n_heads * LANES

    xn = _rms_rows(x_ref[...], g_ref[...]).astype(BF16)
    if fox:
        logf = _log_sigmoid(_dot(xn, e0_ref[...]) + e1_ref[...])
        logf_out[...] = logf
        if final:
            logft_out[...] = logf.T[0:n_heads, :]

    def heads(segment):
        acc = _dot(xn, w_ref[:, segment * w:(segment + 1) * w])
        return [acc[:, h * LANES:(h + 1) * LANES] for h in range(n_heads)]

    if fox:
        def normed(blk, gain):
            return _rms_rows(blk, gain)
    else:
        lane = lax.broadcasted_iota(jnp.int32, (tm, LANES), 1)
        lane_lo = lane < (LANES // 2)
        rot_up = (lane % (LANES // 2)) < (LANES // 4)
        cos = e0_ref[...]
        sin = e1_ref[...]

        def normed(blk, gain):
            sq = blk * blk
            s_lo = jnp.sum(jnp.where(lane_lo, sq, 0.0), axis=-1, keepdims=True)
            s_hi = jnp.sum(jnp.where(lane_lo, 0.0, sq), axis=-1, keepdims=True)
            ms = jnp.where(lane_lo, s_lo, s_hi) * (2.0 / LANES)
            y = blk * lax.rsqrt(ms + EPS) * gain
            up = pltpu.roll(y, LANES - LANES // 4, axis=1)
            dn = pltpu.roll(y, LANES // 4, axis=1)
            return y * cos + jnp.where(rot_up, up, dn) * sin

    def head_lanes(h):
        return slice(h * LANES, (h + 1) * LANES)

    def token_head_rows(h):
        return pl.ds(h, tm, stride=n_heads)

    gain = qn_ref[...]
    for h, blk in enumerate(heads(0)):
        q_out[:, head_lanes(h)] = (normed(blk, gain) * scale).astype(q_out.dtype)

    gain = kn_ref[...]
    for h, blk in enumerate(heads(1)):
        y = normed(blk, gain)
        if not final:
            k_out[:, head_lanes(h)] = y
            continue
        kb_out[:, head_lanes(h)] = y.astype(BF16)
        if fox:
            k_fin[token_head_rows(h), :] = y
        else:
            k_fin[head_lanes(h), :] = y.T

    acc = _dot(xn, w_ref[:, 2 * w:])
    if final:
        vt_out[...] = acc.T.astype(BF16).reshape(vt_out.shape)
        for h in range(n_heads):
            v_fin[token_head_rows(h), :] = acc[:, head_lanes(h)]
    else:
        v_out[...] = acc


def _proj(kind, x, g, w3, qn, kn, extra, *, tm, scale, final, batch=1, depth=1, layer=0,
          prev=()):
    m, d = x.shape
    w = w3.shape[2] // 3
    n_heads = w // LANES
    fox = kind == "fox"
    nb = m // batch // tm
    row = lambda i: (i, 0)
    const = lambda i: (0, 0)
    of_layer = lambda i: (layer, 0, 0)
    in_specs = [
        pl.BlockSpec((tm, d), row),
        pl.BlockSpec((1, d), const),
        pl.BlockSpec((None, d, 3 * w), of_layer, pipeline_mode=pl.Buffered(1)),
        pl.BlockSpec((1, LANES), const),
        pl.BlockSpec((1, LANES), const),
    ]
    if fox:
        in_specs += [pl.BlockSpec((None, d, LANES), of_layer), pl.BlockSpec((1, LANES), const)]
    else:
        tab = lambda i: (i % nb, 0)
        in_specs += [pl.BlockSpec((tm, LANES), tab), pl.BlockSpec((tm, LANES), tab)]
    in_specs += [pl.BlockSpec(memory_space=pl.ANY) for _ in prev]

    rows = (jax.ShapeDtypeStruct((m, w), F32), pl.BlockSpec((tm, w), row))
    rows_bf16 = (jax.ShapeDtypeStruct((m, w), BF16), pl.BlockSpec((tm, w), row))
    logf = (jax.ShapeDtypeStruct((m, LANES), F32), pl.BlockSpec((tm, LANES), row))
    if final:
        seq = m // batch
        slabs, slab0 = (None, layer) if prev else (depth, 0)
        token_head = (jax.ShapeDtypeStruct((depth, m * n_heads, LANES), F32),
                      pl.BlockSpec((slabs, tm * n_heads, LANES), lambda i: (slab0, i, 0)))
        token_minor = lambda r: (jax.ShapeDtypeStruct((depth, batch, r, seq), F32),
                                 pl.BlockSpec((slabs, None, r, tm),
                                              lambda i: (slab0, i // nb, 0, i % nb)))
        vt = (jax.ShapeDtypeStruct((n_heads, m // tm, LANES, tm), BF16),
              pl.BlockSpec((n_heads, 1, LANES, tm), lambda i: (0, i, 0, 0)))
        outs = [rows_bf16, rows_bf16, vt]
        if fox:
            outs += [logf, token_head, token_head, token_minor(n_heads)]
        else:
            outs += [token_minor(w), token_head]
        first_stacked = len(outs) - (3 if fox else 2)
        aliases = {7 + a: first_stacked + a for a in range(len(prev))}
    else:
        outs = [rows, rows, rows] + ([logf] if fox else [])
        aliases = {}
    return pl.pallas_call(
        functools.partial(_proj_kernel, kind=kind, n_heads=n_heads, scale=scale, final=final,
                          layer=layer, n_prev=len(prev)),
        grid=(m // tm,),
        in_specs=in_specs,
        out_specs=[o[1] for o in outs],
        out_shape=[o[0] for o in outs],
        input_output_aliases=aliases,
        compiler_params=_params(("parallel",), 60),
    )(x, g, w3, qn, kn, *extra, *prev)


def _cumaug_kernel(logf_ref, tri_ref, selq_ref, selk_ref, oneq_ref, onek_ref,
                   qa_out, ka_out, carry_sc):
    @pl.when(pl.program_id(1) == 0)
    def _():
        carry_sc[...] = jnp.zeros_like(carry_sc)

    cum = _dot3_left(tri_ref[...], logf_ref[...]) + carry_sc[0:1, :]
    carry_sc[0:1, :] = cum[cum.shape[0] - 1:, :]
    pieces = jnp.concatenate(_split3(cum * LOG2E), axis=1)
    qa_out[...] = (_dot(pieces, selq_ref[...]) + oneq_ref[...]).astype(BF16)
    ka_out[...] = (_dot(pieces, selk_ref[...]) + onek_ref[...]).astype(BF16)


def _aug_constants(n_heads):
    w = n_heads * LANES
    r = jnp.arange(3 * LANES)
    c = jnp.arange(w)
    piece, head = r // LANES, r % LANES
    col_head, col_j = c // LANES, c % LANES
    same = (head[:, None] == col_head[None, :]) & (head[:, None] < n_heads)
    selq = jnp.where(same & (col_j[None, :] == piece[:, None]), 1.0, 0.0).astype(BF16)
    selk = jnp.where(same & (col_j[None, :] == piece[:, None] + 3), -1.0, 0.0).astype(BF16)
    oneq = jnp.where((col_j >= 3) & (col_j < 6), 1.0, 0.0).astype(F32)[None, :]
    onek = jnp.where(col_j < 3, 1.0, 0.0).astype(F32)[None, :]
    return selq, selk, oneq, onek


def _cumaug(logf, batch, n_heads, *, tc):
    m = logf.shape[0]
    t = m // batch
    nt = t // tc
    w = n_heads * LANES
    tri = (jnp.arange(tc)[:, None] >= jnp.arange(tc)[None, :]).astype(BF16)
    selq, selk, oneq, onek = _aug_constants(n_heads)
    const = lambda b, i: (0, 0)
    row = lambda b, i: (b * nt + i, 0)
    return pl.pallas_call(
        _cumaug_kernel,
        grid=(batch, nt),
        in_specs=[
            pl.BlockSpec((tc, LANES), row),
            pl.BlockSpec((tc, tc), const),
            pl.BlockSpec((3 * LANES, w), const),
            pl.BlockSpec((3 * LANES, w), const),
            pl.BlockSpec((1, w), const),
            pl.BlockSpec((1, w), const),
        ],
        out_specs=[pl.BlockSpec((tc, w), row), pl.BlockSpec((tc, w), row)],
        out_shape=[jax.ShapeDtypeStruct((m, w), BF16), jax.ShapeDtypeStruct((m, w), BF16)],
        scratch_shapes=[pltpu.VMEM((8, LANES), F32)],
        compiler_params=_params(("parallel", "arbitrary"), 32),
    )(logf, tri, selq, selk, oneq, onek)


def _lambda(lq1, lk1, lq2, lk2, lam_init):
    a = jnp.sum(lq1[...] * lk1[...], axis=-1, keepdims=True)
    b = jnp.sum(lq2[...] * lk2[...], axis=-1, keepdims=True)
    return jnp.exp(a) - jnp.exp(b) + lam_init


def _flash_kernel(*refs, diff, tq, lam_init):
    if diff:
        (q_ref, k_ref, vt_ref, lq1, lk1, lq2, lk2, sub_ref,
         o_ref, q_sc, m_sc, acc_sc, sa_sc, sb_sc) = refs
    else:
        (q_ref, qa_ref, k_ref, ka_ref, vt_ref,
         o_ref, q_sc, m_sc, acc_sc, sa_sc, sb_sc) = refs
    i = pl.program_id(2)
    cols = q_sc.shape[0]

    if diff:
        q = q_ref[...]
        lane = lax.broadcasted_iota(jnp.int32, q.shape, 1)
        zero = jnp.zeros_like(q)
        q_sc[0:tq, :] = jnp.where(lane < LANES // 2, q, zero)
        q_sc[tq:, :] = jnp.where(lane < LANES // 2, zero, q)
    else:
        q_sc[:, 0:LANES] = q_ref[...]
        q_sc[:, LANES:] = qa_ref[...]
    m_sc[...] = jnp.full_like(m_sc, NEG_INF)
    acc_sc[...] = jnp.zeros_like(acc_sc)
    ones = jnp.ones((SUM_ROWS, tq), BF16)

    chunks = [slice(c * MXU_WIDTH, (c + 1) * MXU_WIDTH) for c in range(cols // MXU_WIDTH)]

    def scores(j, s_sc):
        start = pl.multiple_of(j * tq, tq)
        k = k_ref[pl.ds(start, tq), :]
        if not diff:
            k = jnp.concatenate([k, ka_ref[pl.ds(start, tq), :]], axis=1)
        for cs in chunks:
            s_sc[:, cs] = _dot_nt(k, q_sc[cs, :])

    def accumulate(j, s_sc, masked):
        vt = jnp.concatenate([vt_ref[j], ones], axis=0)
        for c, cs in enumerate(chunks):
            s = s_sc[:, cs]
            if masked:
                key = lax.broadcasted_iota(jnp.int32, s.shape, 0)
                qry = (lax.broadcasted_iota(jnp.int32, s.shape, 1) + c * MXU_WIDTH) % tq
                s = jnp.where(key <= qry, s, NEG_INF)
            m_prev = m_sc[:, cs]
            m_new = jnp.maximum(m_prev, jnp.max(s, axis=0, keepdims=True))
            alpha = jnp.exp2(m_prev - m_new)
            p = jnp.exp2(s - m_new).astype(BF16)
            acc_sc[:, cs] = alpha * acc_sc[:, cs] + _dot(vt, p)
            m_sc[:, cs] = m_new

    scores(0, sa_sc)

    def pair(jj, carry):
        j = 2 * jj
        scores(j + 1, sb_sc)
        accumulate(j, sa_sc, False)
        scores(j + 2, sa_sc)
        accumulate(j + 1, sb_sc, False)
        return carry

    lax.fori_loop(0, i // 2, pair, 0)

    @pl.when(i % 2 == 0)
    def _():
        accumulate(i, sa_sc, True)

    @pl.when(i % 2 == 1)
    def _():
        scores(i, sb_sc)
        accumulate(i - 1, sa_sc, False)
        accumulate(i, sb_sc, True)

    out_t = acc_sc[0:LANES, :] / acc_sc[LANES:LANES + 1, :]
    if diff:
        lam = _lambda(lq1, lk1, lq2, lk2, lam_init)
        o = out_t[:, 0:tq] - lam * out_t[:, tq:]
        ms = jnp.mean(o * o, axis=0, keepdims=True)
        o = o * lax.rsqrt(ms + EPS) * sub_ref[...] * (1.0 - lam_init)
        o_ref[...] = o.T.astype(o_ref.dtype)
    else:
        o_ref[...] = out_t.T.astype(o_ref.dtype)


def _flash(q, k, vt, batch, extra, *, diff, tq, lam_init=0.0):
    m, w = q.shape
    n_heads = w // LANES
    t = m // batch
    nq = t // tq
    qspec = pl.BlockSpec((tq, LANES), lambda b, h, i: (b * nq + i, h))
    kspec = pl.BlockSpec((t, LANES), lambda b, h, i: (b, h))
    vspec = pl.BlockSpec((None, nq, LANES, tq), lambda b, h, i: (h, b, 0, 0))
    if diff:
        vec = pl.BlockSpec((1, LANES // 2), lambda b, h, i: (0, 0))
        sub = pl.BlockSpec((LANES, 1), lambda b, h, i: (0, 0))
        in_specs = [qspec, kspec, vspec, vec, vec, vec, vec, sub]
        args = (q, k, vt) + tuple(extra)
        cols, qw = 2 * tq, LANES
    else:
        qa, ka = extra
        in_specs = [qspec, qspec, kspec, kspec, vspec]
        args = (q, qa, k, ka, vt)
        cols, qw = tq, 2 * LANES
    return pl.pallas_call(
        functools.partial(_flash_kernel, diff=diff, tq=tq, lam_init=lam_init),
        grid=(batch, n_heads, nq),
        in_specs=in_specs,
        out_specs=qspec,
        out_shape=jax.ShapeDtypeStruct((m, w), BF16),
        scratch_shapes=[
            pltpu.VMEM((cols, qw), BF16),
            pltpu.VMEM((1, cols), F32),
            pltpu.VMEM((LANES + SUM_ROWS, cols), F32),
            pltpu.VMEM((tq, cols), F32),
            pltpu.VMEM((tq, cols), F32),
        ],
        compiler_params=_params(("parallel", "parallel", "arbitrary"), 48),
    )(*args)


def _pastsum_kernel(pt_ref, logf_hbm, u_ref, ubig_ref, r_out, buf, sem, *, layer, n_pages):
    b = pl.program_id(0)

    def copy(p):
        page = pt_ref[b * n_pages + p]
        return pltpu.make_async_copy(logf_hbm.at[layer, page], buf.at[p], sem)

    def start(p, c):
        copy(p).start()
        return c

    def wait(p, c):
        copy(p).wait()
        return c

    lax.fori_loop(0, n_pages, start, 0)
    lax.fori_loop(0, n_pages, wait, 0)

    xt = buf[...].reshape(r_out.shape)
    within = _dot3(xt, u_ref[...])
    later = jnp.sum(_dot3_left(ubig_ref[...], xt), axis=1, keepdims=True)
    r_out[...] = within + later


def _pastsum(cache_logf_t, page_table_flat, layer, dec_batch):
    n_pages = page_table_flat.shape[0] // dec_batch
    n_heads, t = cache_logf_t.shape[2:]
    u = (jnp.arange(t)[:, None] > jnp.arange(t)[None, :]).astype(BF16)
    r = jnp.arange(n_pages * n_heads)
    ubig = ((r[None, :] % n_heads == r[:, None] % n_heads)
            & (r[None, :] // n_heads > r[:, None] // n_heads)).astype(BF16)
    rows = n_pages * n_heads
    const = lambda b, pt: (0, 0)
    grid_spec = pltpu.PrefetchScalarGridSpec(
        num_scalar_prefetch=1,
        grid=(dec_batch,),
        in_specs=[
            pl.BlockSpec(memory_space=pl.ANY),
            pl.BlockSpec((t, t), const),
            pl.BlockSpec((rows, rows), const),
        ],
        out_specs=pl.BlockSpec((None, rows, t), lambda b, pt: (b, 0, 0)),
        scratch_shapes=[pltpu.VMEM((n_pages, n_heads, t), F32), pltpu.SemaphoreType.DMA(())],
    )
    return pl.pallas_call(
        functools.partial(_pastsum_kernel, layer=layer, n_pages=n_pages),
        grid_spec=grid_spec,
        out_shape=jax.ShapeDtypeStruct((dec_batch, rows, t), F32),
        compiler_params=_params(("arbitrary",), 32),
    )(page_table_flat, cache_logf_t, u, ubig)


def _decode_kernel(*refs, diff, n_steps, group, n_heads, t_new, page, lam_init):
    refs = list(refs)
    pt_ref, q_ref = refs[0], refs[1]
    k_refs = refs[2:2 + group]
    v_refs = refs[2 + group:2 + 2 * group]
    rest = refs[2 + 2 * group:]
    if diff:
        (kn_ref, vn_ref, lq1, lk1, lq2, lk2, sub_ref,
         o_ref, q_sc, m_sc, l_sc, acc_sc) = rest
    else:
        (kn_ref, vn_ref, r_ref, lw_ref,
         o_ref, q_sc, m_sc, l_sc, acc_sc, qc_sc) = rest
    p = pl.program_id(1)
    rows = q_sc.shape[0]
    groups = 2 * n_heads if diff else n_heads
    group_w = q_sc.shape[1] // groups

    def q_index(shape):
        return lax.broadcasted_iota(jnp.int32, shape, 0) % t_new

    def page_rows(ref):
        heads = [ref[pl.ds(h, page, stride=n_heads), :] for h in range(n_heads)]
        return jnp.concatenate(heads, axis=1).astype(BF16)

    @pl.when(p == 0)
    def _():
        q = jnp.concatenate([q_ref[...]] * (rows // t_new), axis=0)
        r = lax.broadcasted_iota(jnp.int32, q.shape, 0)
        lane = lax.broadcasted_iota(jnp.int32, q.shape, 1)
        if diff:
            grp = ((r // t_new) % n_heads) * 2 + r // (t_new * n_heads)
        else:
            grp = r // t_new
        q_sc[...] = jnp.where(lane // group_w == grp, q, 0.0).astype(BF16)
        m_sc[...] = jnp.full_like(m_sc, NEG_INF)
        l_sc[...] = jnp.zeros_like(l_sc)
        acc_sc[...] = jnp.zeros_like(acc_sc)
        if not diff:
            lw = lw_ref[...]
            i_idx = lax.broadcasted_iota(jnp.int32, lw.shape, 1)
            qc = jnp.sum(jnp.where(i_idx <= q_index(lw.shape), lw, 0.0), axis=1, keepdims=True)
            qc_sc[...] = qc

    def update(s, values):
        m_prev = m_sc[...]
        m_new = jnp.maximum(m_prev, jnp.max(s, axis=-1, keepdims=True))
        alpha = jnp.exp(m_prev - m_new)
        pr = jnp.exp(s - m_new)
        l_sc[...] = alpha * l_sc[...] + jnp.sum(pr, axis=-1, keepdims=True)
        pr = pr.astype(BF16)
        pv = _dot(pr[:, 0:page], values[0])
        for g in range(1, len(values)):
            pv = pv + _dot(pr[:, g * page:(g + 1) * page], values[g])
        acc_sc[...] = alpha * acc_sc[...] + pv
        m_sc[...] = m_new

    @pl.when(p < n_steps)
    def _():
        parts = []
        for g in range(group):
            if diff:
                s = _dot(q_sc[...], k_refs[g][...].astype(BF16))
            else:
                s = _dot_nt(q_sc[...], page_rows(k_refs[g]))
                r8 = r_ref[g * n_heads:(g + 1) * n_heads, :]
                bias = jnp.concatenate(
                    [jnp.broadcast_to(r8[h:h + 1, :], (t_new, page)) for h in range(n_heads)],
                    axis=0)
                s = s + (qc_sc[...] + bias)
            parts.append(s)
        update(jnp.concatenate(parts, axis=1), [page_rows(v) for v in v_refs])

    @pl.when(p == n_steps)
    def _():
        s = _dot_nt(q_sc[...], kn_ref[...].astype(BF16))
        col = lax.broadcasted_iota(jnp.int32, s.shape, 1)
        qi = q_index(s.shape)
        if not diff:
            lw = lw_ref[...]
            bias = jnp.zeros(s.shape, F32)
            for i in range(t_new):
                bias = bias + jnp.where((qi >= i) & (col < i), lw[:, i:i + 1], 0.0)
            s = s + bias
        s = jnp.where(col <= qi, s, NEG_INF)
        update(s, [vn_ref[...].astype(BF16)])

        acc = acc_sc[...]
        inv_l = 1.0 / l_sc[...]
        if diff:
            lam = _lambda(lq1, lk1, lq2, lk2, lam_init)
            half = n_heads * t_new
        for h in range(n_heads):
            lanes = slice(h * LANES, (h + 1) * LANES)
            r0 = h * t_new
            o = acc[r0:r0 + t_new, lanes] * inv_l[r0:r0 + t_new, :]
            if diff:
                o2 = acc[half + r0:half + r0 + t_new, lanes] * inv_l[half + r0:half + r0 + t_new, :]
                o = _rms_rows(o - lam * o2, sub_ref[...]) * (1.0 - lam_init)
            o_ref[:, lanes] = o


def _decode(q, cache_k, cache_v, k_new, v_new, page_table_flat, layer, dec_batch, extra,
            *, diff, group, lam_init=0.0):
    w = q.shape[1]
    t_new = q.shape[0] // dec_batch
    page = k_new.shape[1]
    n_pages = page_table_flat.shape[0] // dec_batch
    n_steps = n_pages // group
    n_heads = w // LANES
    rows = (2 if diff else 1) * n_heads * t_new

    def page_map(g):
        def index(b, p, pt):
            return (layer, pt[b * n_pages + jnp.minimum(p, n_steps - 1) * group + g], 0, 0)
        return index

    per_b = lambda b, p, pt: (b, 0)
    per_b3 = lambda b, p, pt: (b, 0, 0)
    kblock = (None, None) + cache_k.shape[2:]
    vblock = (None, None) + cache_v.shape[2:]
    in_specs = [pl.BlockSpec((t_new, w), per_b)]
    in_specs += [pl.BlockSpec(kblock, page_map(g)) for g in range(group)]
    in_specs += [pl.BlockSpec(vblock, page_map(g)) for g in range(group)]
    in_specs += [pl.BlockSpec((None, page, w), per_b3), pl.BlockSpec((None, page, w), per_b3)]
    scratch = [
        pltpu.VMEM((rows, w), BF16),
        pltpu.VMEM((rows, 1), F32),
        pltpu.VMEM((rows, 1), F32),
        pltpu.VMEM((rows, w), F32),
    ]
    if diff:
        vec = pl.BlockSpec((1, LANES // 2), lambda b, p, pt: (0, 0))
        in_specs += [vec, vec, vec, vec, pl.BlockSpec((1, LANES), lambda b, p, pt: (0, 0))]
    else:
        in_specs += [
            pl.BlockSpec((None, group * n_heads, page),
                         lambda b, p, pt: (b, jnp.minimum(p, n_steps - 1), 0)),
            pl.BlockSpec((None, rows, t_new), per_b3),
        ]
        scratch.append(pltpu.VMEM((rows, 1), F32))
    grid_spec = pltpu.PrefetchScalarGridSpec(
        num_scalar_prefetch=1,
        grid=(dec_batch, n_steps + 1),
        in_specs=in_specs,
        out_specs=pl.BlockSpec((t_new, w), per_b),
        scratch_shapes=scratch,
    )
    return pl.pallas_call(
        functools.partial(_decode_kernel, diff=diff, n_steps=n_steps, group=group,
                          n_heads=n_heads, t_new=t_new, page=page, lam_init=lam_init),
        grid_spec=grid_spec,
        out_shape=jax.ShapeDtypeStruct((dec_batch * t_new, w), F32),
        compiler_params=_params(("parallel", "arbitrary"), 48),
    )(page_table_flat, q, *([cache_k] * group), *([cache_v] * group), k_new, v_new, *extra)


def _oproj_kernel(x_ref, a_ref, b_ref, wa_ref, wb_ref, o_ref):
    o_ref[...] = (x_ref[...] + _dot(a_ref[...].astype(BF16), wa_ref[...])
                  + _dot(b_ref[...].astype(BF16), wb_ref[...]))


def _oproj(x, a, b, w_o, layer, *, tm):
    m, d = x.shape
    w = a.shape[1]
    row = lambda i: (i, 0)
    return pl.pallas_call(
        _oproj_kernel,
        grid=(m // tm,),
        in_specs=[
            pl.BlockSpec((tm, d), row),
            pl.BlockSpec((tm, w), row),
            pl.BlockSpec((tm, w), row),
            pl.BlockSpec((None, w, d), lambda i: (layer, 0, 0)),
            pl.BlockSpec((None, w, d), lambda i: (layer, 1, 0)),
        ],
        out_specs=pl.BlockSpec((tm, d), row),
        out_shape=jax.ShapeDtypeStruct((m, d), F32),
        compiler_params=_params(("parallel",), 48),
    )(x, a, b, w_o, w_o)


def _mlp_kernel(x_ref, g_ref, wu_ref, wd_ref, o_ref, xn_sc, acc_sc):
    j = pl.program_id(1)

    @pl.when(j == 0)
    def _():
        x = x_ref[...]
        xn_sc[...] = _rms_rows(x, g_ref[...]).astype(BF16)
        acc_sc[...] = x

    h = jnp.maximum(_dot(xn_sc[...], wu_ref[...]), 0.0)
    acc_sc[...] += _dot((h * h).astype(BF16), wd_ref[...])

    @pl.when(j == pl.num_programs(1) - 1)
    def _():
        o_ref[...] = acc_sc[...]


def _mlp(x, g, wu, wd, layer, *, tm, tf):
    m, d = x.shape
    f = wu.shape[2]
    row = lambda i, j: (i, 0)
    return pl.pallas_call(
        _mlp_kernel,
        grid=(m // tm, f // tf),
        in_specs=[
            pl.BlockSpec((tm, d), row),
            pl.BlockSpec((1, d), lambda i, j: (0, 0)),
            pl.BlockSpec((None, d, tf), lambda i, j: (layer, 0, j)),
            pl.BlockSpec((None, tf, d), lambda i, j: (layer, j, 0)),
        ],
        out_specs=pl.BlockSpec((tm, d), row),
        out_shape=jax.ShapeDtypeStruct((m, d), F32),
        scratch_shapes=[pltpu.VMEM((tm, d), BF16), pltpu.VMEM((tm, d), F32)],
        compiler_params=_params(("parallel", "arbitrary"), 56),
    )(x, g, wu, wd)


def _rope_tables(pos, dk):
    inv_freq = ROPE_THETA ** (-jnp.arange(0, dk, 2, dtype=F32) / dk)
    ang = pos.astype(F32)[:, None] * inv_freq[None, :]
    cos, sin = jnp.cos(ang), jnp.sin(ang)
    reps = LANES // dk
    cos = jnp.tile(jnp.concatenate([cos, cos], axis=-1), (1, reps))
    sin = jnp.tile(jnp.concatenate([-sin, sin], axis=-1), (1, reps))
    return cos, sin


def _largest_divisor(n, cap):
    return max(d for d in range(1, cap + 1) if n % d == 0)


def kernel(x_prompt, x_sample, cache_fox_k, cache_fox_v, cache_fox_logf, cache_diff_k, cache_diff_v, page_table, norm_attn, w_in, b_forget, fox_q_norm, fox_k_norm, diff_q_norm, diff_k_norm, lambda_q1, lambda_k1, lambda_q2, lambda_k2, diff_subln, w_o, norm_mlp, w_up, w_down):
    depth = w_in.shape[0]
    batch, seq, d_model = x_prompt.shape
    dec_batch, dec_seq, _ = x_sample.shape
    _, n_phys, page, fox_heads, fox_dim = cache_fox_k.shape
    diff_heads, diff_dim = cache_diff_v.shape[3], cache_diff_v.shape[4]
    dk = cache_diff_k.shape[5]
    fox_w = fox_heads * fox_dim
    diff_w = diff_heads * diff_dim
    n_pages = page_table.shape[1]
    past_len = n_pages * page
    assert fox_dim == LANES and diff_dim == LANES and 2 * dk == LANES
    assert seq % ROW_TILE == 0

    c0 = 3 * fox_w
    c1 = c0 + fox_heads
    w_fox = w_in[:, :, :c0].astype(BF16)
    w_gate = jnp.pad(w_in[:, :, c0:c1], ((0, 0), (0, 0), (0, LANES - fox_heads))).astype(BF16)
    b_gate = jnp.pad(b_forget, ((0, 0), (0, LANES - fox_heads)))[:, None, :]
    w_diff = w_in[:, :, c1:].astype(BF16)
    assert fox_w == diff_w
    w_o_b = w_o.astype(BF16)
    w_up_b = w_up.astype(BF16)
    w_down_b = w_down.astype(BF16)
    dqn = jnp.tile(diff_q_norm, (1, 2))[:, None, :]
    dkn = jnp.tile(diff_k_norm, (1, 2))[:, None, :]

    pt_flat = page_table.reshape(-1)
    cache_fk = cache_fox_k.reshape(depth, n_phys, page * fox_heads, fox_dim)
    cache_fv = cache_fox_v.reshape(depth, n_phys, page * fox_heads, fox_dim)
    cache_dv = cache_diff_v.reshape(depth, n_phys, page * diff_heads, diff_dim)
    cache_dk = jnp.transpose(cache_diff_k, (0, 1, 3, 4, 5, 2)).reshape(depth, n_phys, diff_w, page)
    cache_lf = jnp.transpose(cache_fox_logf, (0, 1, 3, 2))
    group = _largest_divisor(n_pages, PAGES_PER_STEP)

    fox_scale = fox_dim ** -0.5
    diff_scale = dk ** -0.5

    def trunk(x, pos, decode):
        m = x.shape[0]
        n_seq = dec_batch if decode else batch
        t = m // n_seq
        tm = m if decode else ROW_TILE
        cos, sin = _rope_tables(pos, dk)
        if decode:
            cos, sin = jnp.tile(cos, (n_seq, 1)), jnp.tile(sin, (n_seq, 1))
        q_unit = 1.0 if decode else LOG2E
        news = []
        fox_stacked, diff_stacked = (), ()
        for l in range(depth):
            lam_init = 0.8 - 0.6 * math.exp(-0.3 * l)
            lam_vecs = (lambda_q1[l][None], lambda_k1[l][None], lambda_q2[l][None],
                        lambda_k2[l][None])
            place = dict(final=not decode, batch=n_seq, depth=depth, layer=l)
            fox = _proj("fox", x, norm_attn[l][None], w_fox, fox_q_norm[l][None],
                        fox_k_norm[l][None], (w_gate, b_gate[l]), tm=tm,
                        scale=fox_scale * q_unit, prev=fox_stacked, **place)
            dif = _proj("diff", x, norm_attn[l][None], w_diff, dqn[l], dkn[l], (cos, sin),
                        tm=tm, scale=diff_scale * q_unit, prev=diff_stacked, **place)
            logf = fox[3]
            if decode:
                fq, fk, fv = fox[:3]
                dq, dkk, dv = dif[:3]
                logf_h = logf[:, :fox_heads]
                pad = lambda a: jnp.pad(a.reshape(n_seq, t, -1), ((0, 0), (0, page - t), (0, 0)))
                rsum = _pastsum(cache_lf, pt_flat, l, n_seq)
                lw = jnp.repeat(logf_h.reshape(n_seq, t, fox_heads).transpose(0, 2, 1), t, axis=1)
                fox_o = _decode(fq, cache_fk, cache_fv, pad(fk), pad(fv), pt_flat, l, n_seq,
                                (rsum, lw), diff=False, group=group)
                diff_o = _decode(dq, cache_dk, cache_dv, pad(dkk), pad(dv), pt_flat, l, n_seq,
                                 lam_vecs + (diff_subln[l][None],), diff=True, group=group,
                                 lam_init=lam_init)
                news.append((fk, fv, logf_h, dkk, dv))
            else:
                fq, fkb, fvt = fox[:3]
                dq, dkb, dvt = dif[:3]
                fox_stacked, diff_stacked = tuple(fox[4:]), tuple(dif[3:])
                qa, ka = _cumaug(logf, n_seq, fox_heads, tc=ROW_TILE)
                fox_o = _flash(fq, fkb, fvt, n_seq, (qa, ka), diff=False, tq=ROW_TILE)
                diff_o = _flash(dq, dkb, dvt, n_seq, lam_vecs + (diff_subln[l][:, None],),
                                diff=True, tq=ROW_TILE, lam_init=lam_init)
            x = _oproj(x, fox_o, diff_o, w_o_b, l, tm=tm)
            x = _mlp(x, norm_mlp[l][None], w_up_b, w_down_b, l, tm=tm, tf=MLP_FF_TILE)
        lead = (depth, n_seq, t)
        if decode:
            stack = lambda i, shape: jnp.stack([n[i] for n in news]).reshape(lead + shape)
            new_cache = (stack(0, (fox_heads, fox_dim)), stack(1, (fox_heads, fox_dim)),
                         stack(2, (fox_heads,)), stack(3, (diff_heads, 2, dk)),
                         stack(4, (diff_heads, diff_dim)))
        else:
            fk, fv, logf_t = fox_stacked
            dk_t, dv = diff_stacked
            new_cache = (fk.reshape(lead + (fox_heads, fox_dim)),
                         fv.reshape(lead + (fox_heads, fox_dim)),
                         jnp.transpose(logf_t, (0, 1, 3, 2)),
                         jnp.transpose(dk_t.reshape(depth, n_seq, diff_heads, 2, dk, t),
                                       (0, 1, 5, 2, 3, 4)),
                         dv.reshape(lead + (diff_heads, diff_dim)))
        return (x.reshape(n_seq, t, d_model),) + new_cache

    pos_prompt = jnp.arange(seq, dtype=jnp.int32)
    pos_sample = past_len + jnp.arange(dec_seq, dtype=jnp.int32)
    yp = trunk(x_prompt.reshape(batch * seq, d_model), pos_prompt, False)
    ys = trunk(x_sample.reshape(dec_batch * dec_seq, d_model), pos_sample, True)
    return (yp[0], ys[0]) + yp[1:] + ys[1:]
```

```python
import functools
import math

import jax
import jax.numpy as jnp
from jax import lax
from jax.experimental import pallas as pl
from jax.experimental.pallas import tpu as pltpu

F32 = jnp.float32
BF16 = jnp.bfloat16
EPS = 1e-6
ROPE_THETA = 10000.0
NEG_INF = float("-inf")
MIB = 1024 * 1024

LANES = 128
MXU_WIDTH = 256
ROW_TILE = 512
MLP_FF_TILE = 1024
PAGES_PER_STEP = 16
SUM_ROWS = 16
LOG2E = math.log2(math.e)


def _params(sem, vmem_mib):
    return pltpu.CompilerParams(dimension_semantics=sem, vmem_limit_bytes=vmem_mib * MIB)


def _split3(x):
    hi = x.astype(BF16)
    r1 = x - hi.astype(F32)
    mid = r1.astype(BF16)
    lo = (r1 - mid.astype(F32)).astype(BF16)
    return hi, mid, lo


def _dot(a, b):
    return jnp.dot(a, b, preferred_element_type=F32)


def _dot_nt(a, b):
    return lax.dot_general(a, b, (((1,), (1,)), ((), ())), preferred_element_type=F32)


def _dot3(x, w):
    hi, mid, lo = _split3(x)
    return _dot(hi, w) + _dot(mid, w) + _dot(lo, w)


def _dot3_left(w, x):
    hi, mid, lo = _split3(x)
    return _dot(w, hi) + _dot(w, mid) + _dot(w, lo)


def _log_sigmoid(x):
    return jnp.minimum(x, 0.0) - jnp.log1p(jnp.exp(-jnp.abs(x)))


def _rms_rows(x, g):
    ms = jnp.mean(x * x, axis=-1, keepdims=True)
    return x * lax.rsqrt(ms + EPS) * g


def _proj_kernel(*refs, kind, n_heads, scale, final, layer, n_prev):
    x_ref, g_ref, w_ref, qn_ref, kn_ref, e0_ref, e1_ref = refs[:7]
    outs = refs[7 + n_prev:]
    fox = kind == "fox"
    if final:
        q_out, kb_out, vt_out = outs[:3]
        stacked = list(outs[4:] if fox else outs[3:])
        if n_prev == 0:
            for i, ref in enumerate(stacked):
                for other in range(ref.shape[0]):
                    if other != layer:
                        ref[other] = jnp.zeros(ref.shape[1:], F32)
                stacked[i] = ref.at[layer]
        if fox:
            logf_out = outs[3]
            k_fin, v_fin, logft_out = stacked
        else:
            k_fin, v_fin = stacked
    else:
        q_out, k_out, v_out = outs[:3]
        if fox:
            logf_out = outs[3]
    tm = x_ref.shape[0]
    w = n_heads * LANES

    xn = _rms_rows(x_ref[...], g_ref[...]).astype(BF16)
    if fox:
        logf = _log_sigmoid(_dot(xn, e0_ref[...]) + e1_ref[...])
        logf_out[...] = logf
        if final:
            logft_out[...] = logf.T[0:n_heads, :]

    def heads(segment):
        acc = _dot(xn, w_ref[:, segment * w:(segment + 1) * w])
        return [acc[:, h * LANES:(h + 1) * LANES] for h in range(n_heads)]

    if fox:
        def normed(blk, gain):
            return _rms_rows(blk, gain)
    else:
        lane = lax.broadcasted_iota(jnp.int32, (tm, LANES), 1)
        lane_lo = lane < (LANES // 2)
        rot_up = (lane % (LANES // 2)) < (LANES // 4)
        cos = e0_ref[...]
        sin = e1_ref[...]

        def normed(blk, gain):
            sq = blk * blk
            s_lo = jnp.sum(jnp.where(lane_lo, sq, 0.0), axis=-1, keepdims=True)
            s_hi = jnp.sum(jnp.where(lane_lo, 0.0, sq), axis=-1, keepdims=True)
            ms = jnp.where(lane_lo, s_lo, s_hi) * (2.0 / LANES)
            y = blk * lax.rsqrt(ms + EPS) * gain
            up = pltpu.roll(y, LANES - LANES // 4, axis=1)
            dn = pltpu.roll(y, LANES // 4, axis=1)
            return y * cos + jnp.where(rot_up, up, dn) * sin

    def head_lanes(h):
        return slice(h * LANES, (h + 1) * LANES)

    def token_head_rows(h):
        return pl.ds(h, tm, stride=n_heads)

    gain = qn_ref[...]
    for h, blk in enumerate(heads(0)):
        q_out[:, head_lanes(h)] = (normed(blk, gain) * scale).astype(q_out.dtype)

    gain = kn_ref[...]
    for h, blk in enumerate(heads(1)):
        y = normed(blk, gain)
        if not final:
            k_out[:, head_lanes(h)] = y
            continue
        kb_out[:, head_lanes(h)] = y.astype(BF16)
        if fox:
            k_fin[token_head_rows(h), :] = y
        else:
            k_fin[head_lanes(h), :] = y.T

    acc = _dot(xn, w_ref[:, 2 * w:])
    if final:
        vt_out[...] = acc.T.astype(BF16).reshape(vt_out.shape)
        for h in range(n_heads):
            v_fin[token_head_rows(h), :] = acc[:, head_lanes(h)]
    else:
        v_out[...] = acc


def _proj(kind, x, g, w3, qn, kn, extra, *, tm, scale, final, batch=1, depth=1, layer=0,
          prev=()):
    m, d = x.shape
    w = w3.shape[2] // 3
    n_heads = w // LANES
    fox = kind == "fox"
    nb = m // batch // tm
    row = lambda i: (i, 0)
    const = lambda i: (0, 0)
    of_layer = lambda i: (layer, 0, 0)
    in_specs = [
        pl.BlockSpec((tm, d), row),
        pl.BlockSpec((1, d), const),
        pl.BlockSpec((None, d, 3 * w), of_layer, pipeline_mode=pl.Buffered(1)),
        pl.BlockSpec((1, LANES), const),
        pl.BlockSpec((1, LANES), const),
    ]
    if fox:
        in_specs += [pl.BlockSpec((None, d, LANES), of_layer), pl.BlockSpec((1, LANES), const)]
    else:
        tab = lambda i: (i % nb, 0)
        in_specs += [pl.BlockSpec((tm, LANES), tab), pl.BlockSpec((tm, LANES), tab)]
    in_specs += [pl.BlockSpec(memory_space=pl.ANY) for _ in prev]

    rows = (jax.ShapeDtypeStruct((m, w), F32), pl.BlockSpec((tm, w), row))
    rows_bf16 = (jax.ShapeDtypeStruct((m, w), BF16), pl.BlockSpec((tm, w), row))
    logf = (jax.ShapeDtypeStruct((m, LANES), F32), pl.BlockSpec((tm, LANES), row))
    if final:
        seq = m // batch
        slabs, slab0 = (None, layer) if prev else (depth, 0)
        token_head = (jax.ShapeDtypeStruct((depth, m * n_heads, LANES), F32),
                      pl.BlockSpec((slabs, tm * n_heads, LANES), lambda i: (slab0, i, 0)))
        token_minor = lambda r: (jax.ShapeDtypeStruct((depth, batch, r, seq), F32),
                                 pl.BlockSpec((slabs, None, r, tm),
                                              lambda i: (slab0, i // nb, 0, i % nb)))
        vt = (jax.ShapeDtypeStruct((n_heads, m // tm, LANES, tm), BF16),
              pl.BlockSpec((n_heads, 1, LANES, tm), lambda i: (0, i, 0, 0)))
        outs = [rows_bf16, rows_bf16, vt]
        if fox:
            outs += [logf, token_head, token_head, token_minor(n_heads)]
        else:
            outs += [token_minor(w), token_head]
        first_stacked = len(outs) - (3 if fox else 2)
        aliases = {7 + a: first_stacked + a for a in range(len(prev))}
    else:
        outs = [rows, rows, rows] + ([logf] if fox else [])
        aliases = {}
    return pl.pallas_call(
        functools.partial(_proj_kernel, kind=kind, n_heads=n_heads, scale=scale, final=final,
                          layer=layer, n_prev=len(prev)),
        grid=(m // tm,),
        in_specs=in_specs,
        out_specs=[o[1] for o in outs],
        out_shape=[o[0] for o in outs],
        input_output_aliases=aliases,
        compiler_params=_params(("parallel",), 60),
    )(x, g, w3, qn, kn, *extra, *prev)


def _cumaug_kernel(logf_ref, tri_ref, selq_ref, selk_ref, oneq_ref, onek_ref,
                   qa_out, ka_out, carry_sc):
    @pl.when(pl.program_id(1) == 0)
    def _():
        carry_sc[...] = jnp.zeros_like(carry_sc)

    cum = _dot3_left(tri_ref[...], logf_ref[...]) + carry_sc[0:1, :]
    carry_sc[0:1, :] = cum[cum.shape[0] - 1:, :]
    pieces = jnp.concatenate(_split3(cum * LOG2E), axis=1)
    qa_out[...] = (_dot(pieces, selq_ref[...]) + oneq_ref[...]).astype(BF16)
    ka_out[...] = (_dot(pieces, selk_ref[...]) + onek_ref[...]).astype(BF16)


def _aug_constants(n_heads):
    w = n_heads * LANES
    r = jnp.arange(3 * LANES)
    c = jnp.arange(w)
    piece, head = r // LANES, r % LANES
    col_head, col_j = c // LANES, c % LANES
    same = (head[:, None] == col_head[None, :]) & (head[:, None] < n_heads)
    selq = jnp.where(same & (col_j[None, :] == piece[:, None]), 1.0, 0.0).astype(BF16)
    selk = jnp.where(same & (col_j[None, :] == piece[:, None] + 3), -1.0, 0.0).astype(BF16)
    oneq = jnp.where((col_j >= 3) & (col_j < 6), 1.0, 0.0).astype(F32)[None, :]
    onek = jnp.where(col_j < 3, 1.0, 0.0).astype(F32)[None, :]
    return selq, selk, oneq, onek


def _cumaug(logf, batch, n_heads, *, tc):
    m = logf.shape[0]
    t = m // batch
    nt = t // tc
    w = n_heads * LANES
    tri = (jnp.arange(tc)[:, None] >= jnp.arange(tc)[None, :]).astype(BF16)
    selq, selk, oneq, onek = _aug_constants(n_heads)
    const = lambda b, i: (0, 0)
    row = lambda b, i: (b * nt + i, 0)
    return pl.pallas_call(
        _cumaug_kernel,
        grid=(batch, nt),
        in_specs=[
            pl.BlockSpec((tc, LANES), row),
            pl.BlockSpec((tc, tc), const),
            pl.BlockSpec((3 * LANES, w), const),
            pl.BlockSpec((3 * LANES, w), const),
            pl.BlockSpec((1, w), const),
            pl.BlockSpec((1, w), const),
        ],
        out_specs=[pl.BlockSpec((tc, w), row), pl.BlockSpec((tc, w), row)],
        out_shape=[jax.ShapeDtypeStruct((m, w), BF16), jax.ShapeDtypeStruct((m, w), BF16)],
        scratch_shapes=[pltpu.VMEM((8, LANES), F32)],
        compiler_params=_params(("parallel", "arbitrary"), 32),
    )(logf, tri, selq, selk, oneq, onek)


def _lambda(lq1, lk1, lq2, lk2, lam_init):
    a = jnp.sum(lq1[...] * lk1[...], axis=-1, keepdims=True)
    b = jnp.sum(lq2[...] * lk2[...], axis=-1, keepdims=True)
    return jnp.exp(a) - jnp.exp(b) + lam_init


def _flash_kernel(*refs, diff, tq, lam_init):
    if diff:
        (q_ref, k_ref, vt_ref, lq1, lk1, lq2, lk2, sub_ref,
         o_ref, q_sc, m_sc, acc_sc, s_sc) = refs
    else:
        (q_ref, qa_ref, k_ref, ka_ref, vt_ref,
         o_ref, q_sc, m_sc, acc_sc, s_sc) = refs
    n_tiles = q_ref.shape[0] // tq
    cols = q_sc.shape[1]
    chunks = [slice(c * MXU_WIDTH, (c + 1) * MXU_WIDTH) for c in range(cols // MXU_WIDTH)]
    ones = jnp.ones((SUM_ROWS, tq), BF16)
    if diff:
        lam = _lambda(lq1, lk1, lq2, lk2, lam_init)

    def prepare(i):
        par = i
        rows = slice(i * tq, (i + 1) * tq)
        if diff:
            q = q_ref[rows, :]
            lane = lax.broadcasted_iota(jnp.int32, q.shape, 1)
            zero = jnp.zeros_like(q)
            q_sc[par, 0:tq, :] = jnp.where(lane < LANES // 2, q, zero)
            q_sc[par, tq:, :] = jnp.where(lane < LANES // 2, zero, q)
        else:
            q_sc[par, :, 0:LANES] = q_ref[rows, :]
            q_sc[par, :, LANES:] = qa_ref[rows, :]
        m_sc[par] = jnp.full(m_sc.shape[1:], NEG_INF, F32)
        acc_sc[par] = jnp.zeros(acc_sc.shape[1:], F32)

    def visible_keys(c, diagonal):
        return (c * MXU_WIDTH) % tq + MXU_WIDTH if diagonal else tq

    def scores(i, j, slot, diagonal=False):
        start = j * tq if isinstance(j, int) else pl.multiple_of(j * tq, tq)
        k = k_ref[pl.ds(start, tq), :]
        if not diff:
            k = jnp.concatenate([k, ka_ref[pl.ds(start, tq), :]], axis=1)
        for c, cs in enumerate(chunks):
            n = visible_keys(c, diagonal)
            s_sc[slot, 0:n, cs] = _dot_nt(k[0:n], q_sc[i, cs, :])

    def accumulate(i, j, slot, diagonal):
        par = i
        vt = jnp.concatenate([vt_ref[j], ones], axis=0)
        for c, cs in enumerate(chunks):
            n = visible_keys(c, diagonal)
            s = s_sc[slot, 0:n, cs]
            if diagonal:
                key = lax.broadcasted_iota(jnp.int32, s.shape, 0)
                qry = (lax.broadcasted_iota(jnp.int32, s.shape, 1) + c * MXU_WIDTH) % tq
                s = jnp.where(key <= qry, s, NEG_INF)
            m_prev = m_sc[par, :, cs]
            m_new = jnp.maximum(m_prev, jnp.max(s, axis=0, keepdims=True))
            alpha = jnp.exp2(m_prev - m_new)
            p = jnp.exp2(s - m_new).astype(BF16)
            acc_sc[par, :, cs] = alpha * acc_sc[par, :, cs] + _dot(vt[:, 0:n], p)
            m_sc[par, :, cs] = m_new

    def finalize(i):
        par = i
        out_t = acc_sc[par, 0:LANES, :] / acc_sc[par, LANES:LANES + 1, :]
        if diff:
            o = out_t[:, 0:tq] - lam * out_t[:, tq:]
            ms = jnp.mean(o * o, axis=0, keepdims=True)
            out_t = o * lax.rsqrt(ms + EPS) * sub_ref[...] * (1.0 - lam_init)
        o_ref[i * tq:(i + 1) * tq, :] = out_t.T.astype(o_ref.dtype)

    prepare(0)
    scores(0, 0, 0, diagonal=True)
    first = 0
    for i in range(n_tiles):
        a, b = first, 1 - first

        def pair(jj, carry, i=i, a=a, b=b):
            j = 2 * jj
            scores(i, j + 1, b)
            accumulate(i, j, a, False)
            scores(i, j + 2, a)
            accumulate(i, j + 1, b, False)
            return carry

        if i // 2:
            lax.fori_loop(0, i // 2, pair, 0)
        more = i + 1 < n_tiles
        if i % 2 == 0:
            if more:
                prepare(i + 1)
                scores(i + 1, 0, b)
            accumulate(i, i, a, True)
            first = b
        else:
            scores(i, i, b, diagonal=True)
            accumulate(i, i - 1, a, False)
            if more:
                prepare(i + 1)
                scores(i + 1, 0, a)
            accumulate(i, i, b, True)
            first = a
        finalize(i)


def _flash(q, k, vt, batch, extra, *, diff, tq, lam_init=0.0):
    m, w = q.shape
    n_heads = w // LANES
    t = m // batch
    nq = t // tq
    seq = pl.BlockSpec((t, LANES), lambda b, h: (b, h))
    vspec = pl.BlockSpec((None, nq, LANES, tq), lambda b, h: (h, b, 0, 0))
    if diff:
        vec = pl.BlockSpec((1, LANES // 2), lambda b, h: (0, 0))
        sub = pl.BlockSpec((LANES, 1), lambda b, h: (0, 0))
        in_specs = [seq, seq, vspec, vec, vec, vec, vec, sub]
        args = (q, k, vt) + tuple(extra)
        cols, qw = 2 * tq, LANES
    else:
        qa, ka = extra
        in_specs = [seq, seq, seq, seq, vspec]
        args = (q, qa, k, ka, vt)
        cols, qw = tq, 2 * LANES
    return pl.pallas_call(
        functools.partial(_flash_kernel, diff=diff, tq=tq, lam_init=lam_init),
        grid=(batch, n_heads),
        in_specs=in_specs,
        out_specs=seq,
        out_shape=jax.ShapeDtypeStruct((m, w), BF16),
        scratch_shapes=[
            pltpu.VMEM((nq, cols, qw), BF16),
            pltpu.VMEM((nq, 1, cols), F32),
            pltpu.VMEM((nq, LANES + SUM_ROWS, cols), F32),
            pltpu.VMEM((2, tq, cols), F32),
        ],
        compiler_params=_params(("parallel", "parallel"), 48),
    )(*args)


def _pastsum_kernel(pt_ref, logf_hbm, u_ref, ubig_ref, r_out, buf, sem, *, layer, n_pages):
    b = pl.program_id(0)

    def copy(p):
        page = pt_ref[b * n_pages + p]
        return pltpu.make_async_copy(logf_hbm.at[layer, page], buf.at[p], sem)

    def start(p, c):
        copy(p).start()
        return c

    def wait(p, c):
        copy(p).wait()
        return c

    lax.fori_loop(0, n_pages, start, 0)
    lax.fori_loop(0, n_pages, wait, 0)

    xt = buf[...].reshape(r_out.shape)
    within = _dot3(xt, u_ref[...])
    later = jnp.sum(_dot3_left(ubig_ref[...], xt), axis=1, keepdims=True)
    r_out[...] = within + later


def _pastsum(cache_logf_t, page_table_flat, layer, dec_batch):
    n_pages = page_table_flat.shape[0] // dec_batch
    n_heads, t = cache_logf_t.shape[2:]
    u = (jnp.arange(t)[:, None] > jnp.arange(t)[None, :]).astype(BF16)
    r = jnp.arange(n_pages * n_heads)
    ubig = ((r[None, :] % n_heads == r[:, None] % n_heads)
            & (r[None, :] // n_heads > r[:, None] // n_heads)).astype(BF16)
    rows = n_pages * n_heads
    const = lambda b, pt: (0, 0)
    grid_spec = pltpu.PrefetchScalarGridSpec(
        num_scalar_prefetch=1,
        grid=(dec_batch,),
        in_specs=[
            pl.BlockSpec(memory_space=pl.ANY),
            pl.BlockSpec((t, t), const),
            pl.BlockSpec((rows, rows), const),
        ],
        out_specs=pl.BlockSpec((None, rows, t), lambda b, pt: (b, 0, 0)),
        scratch_shapes=[pltpu.VMEM((n_pages, n_heads, t), F32), pltpu.SemaphoreType.DMA(())],
    )
    return pl.pallas_call(
        functools.partial(_pastsum_kernel, layer=layer, n_pages=n_pages),
        grid_spec=grid_spec,
        out_shape=jax.ShapeDtypeStruct((dec_batch, rows, t), F32),
        compiler_params=_params(("arbitrary",), 32),
    )(page_table_flat, cache_logf_t, u, ubig)


def _decode_kernel(*refs, diff, n_steps, group, n_heads, t_new, page, lam_init):
    refs = list(refs)
    pt_ref, q_ref = refs[0], refs[1]
    k_refs = refs[2:2 + group]
    v_refs = refs[2 + group:2 + 2 * group]
    rest = refs[2 + 2 * group:]
    if diff:
        (kn_ref, vn_ref, lq1, lk1, lq2, lk2, sub_ref,
         o_ref, q_sc, m_sc, l_sc, acc_sc) = rest
    else:
        (kn_ref, vn_ref, r_ref, lw_ref,
         o_ref, q_sc, m_sc, l_sc, acc_sc, qc_sc) = rest
    p = pl.program_id(1)
    rows = q_sc.shape[0]
    groups = 2 * n_heads if diff else n_heads
    group_w = q_sc.shape[1] // groups

    def q_index(shape):
        return lax.broadcasted_iota(jnp.int32, shape, 0) % t_new

    def page_rows(ref):
        heads = [ref[pl.ds(h, page, stride=n_heads), :] for h in range(n_heads)]
        return jnp.concatenate(heads, axis=1).astype(BF16)

    @pl.when(p == 0)
    def _():
        q = jnp.concatenate([q_ref[...]] * (rows // t_new), axis=0)
        r = lax.broadcasted_iota(jnp.int32, q.shape, 0)
        lane = lax.broadcasted_iota(jnp.int32, q.shape, 1)
        if diff:
            grp = ((r // t_new) % n_heads) * 2 + r // (t_new * n_heads)
        else:
            grp = r // t_new
        q_sc[...] = jnp.where(lane // group_w == grp, q, 0.0).astype(BF16)
        m_sc[...] = jnp.full_like(m_sc, NEG_INF)
        l_sc[...] = jnp.zeros_like(l_sc)
        acc_sc[...] = jnp.zeros_like(acc_sc)
        if not diff:
            lw = lw_ref[...]
            i_idx = lax.broadcasted_iota(jnp.int32, lw.shape, 1)
            qc = jnp.sum(jnp.where(i_idx <= q_index(lw.shape), lw, 0.0), axis=1, keepdims=True)
            qc_sc[...] = qc

    def update(s, values):
        m_prev = m_sc[...]
        m_new = jnp.maximum(m_prev, jnp.max(s, axis=-1, keepdims=True))
        alpha = jnp.exp(m_prev - m_new)
        pr = jnp.exp(s - m_new)
        l_sc[...] = alpha * l_sc[...] + jnp.sum(pr, axis=-1, keepdims=True)
        pr = pr.astype(BF16)
        pv = _dot(pr[:, 0:page], values[0])
        for g in range(1, len(values)):
            pv = pv + _dot(pr[:, g * page:(g + 1) * page], values[g])
        acc_sc[...] = alpha * acc_sc[...] + pv
        m_sc[...] = m_new

    @pl.when(p < n_steps)
    def _():
        parts = []
        for g in range(group):
            if diff:
                s = _dot(q_sc[...], k_refs[g][...].astype(BF16))
            else:
                s = _dot_nt(q_sc[...], page_rows(k_refs[g]))
                r8 = r_ref[g * n_heads:(g + 1) * n_heads, :]
                bias = jnp.concatenate(
                    [jnp.broadcast_to(r8[h:h + 1, :], (t_new, page)) for h in range(n_heads)],
                    axis=0)
                s = s + (qc_sc[...] + bias)
            parts.append(s)
        update(jnp.concatenate(parts, axis=1), [page_rows(v) for v in v_refs])

    @pl.when(p == n_steps)
    def _():
        s = _dot_nt(q_sc[...], kn_ref[...].astype(BF16))
        col = lax.broadcasted_iota(jnp.int32, s.shape, 1)
        qi = q_index(s.shape)
        if not diff:
            lw = lw_ref[...]
            bias = jnp.zeros(s.shape, F32)
            for i in range(t_new):
                bias = bias + jnp.where((qi >= i) & (col < i), lw[:, i:i + 1], 0.0)
            s = s + bias
        s = jnp.where(col <= qi, s, NEG_INF)
        update(s, [vn_ref[...].astype(BF16)])

        acc = acc_sc[...]
        inv_l = 1.0 / l_sc[...]
        if diff:
            lam = _lambda(lq1, lk1, lq2, lk2, lam_init)
            half = n_heads * t_new
        for h in range(n_heads):
            lanes = slice(h * LANES, (h + 1) * LANES)
            r0 = h * t_new
            o = acc[r0:r0 + t_new, lanes] * inv_l[r0:r0 + t_new, :]
            if diff:
                o2 = acc[half + r0:half + r0 + t_new, lanes] * inv_l[half + r0:half + r0 + t_new, :]
                o = _rms_rows(o - lam * o2, sub_ref[...]) * (1.0 - lam_init)
            o_ref[:, lanes] = o


def _decode(q, cache_k, cache_v, k_new, v_new, page_table_flat, layer, dec_batch, extra,
            *, diff, group, lam_init=0.0):
    w = q.shape[1]
    t_new = q.shape[0] // dec_batch
    page = k_new.shape[1]
    n_pages = page_table_flat.shape[0] // dec_batch
    n_steps = n_pages // group
    n_heads = w // LANES
    rows = (2 if diff else 1) * n_heads * t_new

    def page_map(g):
        def index(b, p, pt):
            return (layer, pt[b * n_pages + jnp.minimum(p, n_steps - 1) * group + g], 0, 0)
        return index

    per_b = lambda b, p, pt: (b, 0)
    per_b3 = lambda b, p, pt: (b, 0, 0)
    kblock = (None, None) + cache_k.shape[2:]
    vblock = (None, None) + cache_v.shape[2:]
    in_specs = [pl.BlockSpec((t_new, w), per_b)]
    in_specs += [pl.BlockSpec(kblock, page_map(g)) for g in range(group)]
    in_specs += [pl.BlockSpec(vblock, page_map(g)) for g in range(group)]
    in_specs += [pl.BlockSpec((None, page, w), per_b3), pl.BlockSpec((None, page, w), per_b3)]
    scratch = [
        pltpu.VMEM((rows, w), BF16),
        pltpu.VMEM((rows, 1), F32),
        pltpu.VMEM((rows, 1), F32),
        pltpu.VMEM((rows, w), F32),
    ]
    if diff:
        vec = pl.BlockSpec((1, LANES // 2), lambda b, p, pt: (0, 0))
        in_specs += [vec, vec, vec, vec, pl.BlockSpec((1, LANES), lambda b, p, pt: (0, 0))]
    else:
        in_specs += [
            pl.BlockSpec((None, group * n_heads, page),
                         lambda b, p, pt: (b, jnp.minimum(p, n_steps - 1), 0)),
            pl.BlockSpec((None, rows, t_new), per_b3),
        ]
        scratch.append(pltpu.VMEM((rows, 1), F32))
    grid_spec = pltpu.PrefetchScalarGridSpec(
        num_scalar_prefetch=1,
        grid=(dec_batch, n_steps + 1),
        in_specs=in_specs,
        out_specs=pl.BlockSpec((t_new, w), per_b),
        scratch_shapes=scratch,
    )
    return pl.pallas_call(
        functools.partial(_decode_kernel, diff=diff, n_steps=n_steps, group=group,
                          n_heads=n_heads, t_new=t_new, page=page, lam_init=lam_init),
        grid_spec=grid_spec,
        out_shape=jax.ShapeDtypeStruct((dec_batch * t_new, w), F32),
        compiler_params=_params(("parallel", "arbitrary"), 48),
    )(page_table_flat, q, *([cache_k] * group), *([cache_v] * group), k_new, v_new, *extra)


def _oproj_kernel(x_ref, a_ref, b_ref, wa_ref, wb_ref, o_ref):
    o_ref[...] = (x_ref[...] + _dot(a_ref[...].astype(BF16), wa_ref[...])
                  + _dot(b_ref[...].astype(BF16), wb_ref[...]))


def _oproj(x, a, b, w_o, layer, *, tm):
    m, d = x.shape
    w = a.shape[1]
    row = lambda i: (i, 0)
    return pl.pallas_call(
        _oproj_kernel,
        grid=(m // tm,),
        in_specs=[
            pl.BlockSpec((tm, d), row),
            pl.BlockSpec((tm, w), row),
            pl.BlockSpec((tm, w), row),
            pl.BlockSpec((None, w, d), lambda i: (layer, 0, 0)),
            pl.BlockSpec((None, w, d), lambda i: (layer, 1, 0)),
        ],
        out_specs=pl.BlockSpec((tm, d), row),
        out_shape=jax.ShapeDtypeStruct((m, d), F32),
        compiler_params=_params(("parallel",), 48),
    )(x, a, b, w_o, w_o)


def _mlp_kernel(x_ref, g_ref, wu_ref, wd_ref, o_ref, xn_sc, acc_sc):
    j = pl.program_id(1)

    @pl.when(j == 0)
    def _():
        x = x_ref[...]
        xn_sc[...] = _rms_rows(x, g_ref[...]).astype(BF16)
        acc_sc[...] = x

    h = jnp.maximum(_dot(xn_sc[...], wu_ref[...]), 0.0)
    acc_sc[...] += _dot((h * h).astype(BF16), wd_ref[...])

    @pl.when(j == pl.num_programs(1) - 1)
    def _():
        o_ref[...] = acc_sc[...]


def _mlp(x, g, wu, wd, layer, *, tm, tf):
    m, d = x.shape
    f = wu.shape[2]
    row = lambda i, j: (i, 0)
    return pl.pallas_call(
        _mlp_kernel,
        grid=(m // tm, f // tf),
        in_specs=[
            pl.BlockSpec((tm, d), row),
            pl.BlockSpec((1, d), lambda i, j: (0, 0)),
            pl.BlockSpec((None, d, tf), lambda i, j: (layer, 0, j)),
            pl.BlockSpec((None, tf, d), lambda i, j: (layer, j, 0)),
        ],
        out_specs=pl.BlockSpec((tm, d), row),
        out_shape=jax.ShapeDtypeStruct((m, d), F32),
        scratch_shapes=[pltpu.VMEM((tm, d), BF16), pltpu.VMEM((tm, d), F32)],
        compiler_params=_params(("parallel", "arbitrary"), 56),
    )(x, g, wu, wd)


def _rope_tables(pos, dk):
    inv_freq = ROPE_THETA ** (-jnp.arange(0, dk, 2, dtype=F32) / dk)
    ang = pos.astype(F32)[:, None] * inv_freq[None, :]
    cos, sin = jnp.cos(ang), jnp.sin(ang)
    reps = LANES // dk
    cos = jnp.tile(jnp.concatenate([cos, cos], axis=-1), (1, reps))
    sin = jnp.tile(jnp.concatenate([-sin, sin], axis=-1), (1, reps))
    return cos, sin


def _largest_divisor(n, cap):
    return max(d for d in range(1, cap + 1) if n % d == 0)


def kernel(x_prompt, x_sample, cache_fox_k, cache_fox_v, cache_fox_logf, cache_diff_k, cache_diff_v, page_table, norm_attn, w_in, b_forget, fox_q_norm, fox_k_norm, diff_q_norm, diff_k_norm, lambda_q1, lambda_k1, lambda_q2, lambda_k2, diff_subln, w_o, norm_mlp, w_up, w_down):
    depth = w_in.shape[0]
    batch, seq, d_model = x_prompt.shape
    dec_batch, dec_seq, _ = x_sample.shape
    _, n_phys, page, fox_heads, fox_dim = cache_fox_k.shape
    diff_heads, diff_dim = cache_diff_v.shape[3], cache_diff_v.shape[4]
    dk = cache_diff_k.shape[5]
    fox_w = fox_heads * fox_dim
    diff_w = diff_heads * diff_dim
    n_pages = page_table.shape[1]
    past_len = n_pages * page
    assert fox_dim == LANES and diff_dim == LANES and 2 * dk == LANES
    assert seq % ROW_TILE == 0

    c0 = 3 * fox_w
    c1 = c0 + fox_heads
    w_fox = w_in[:, :, :c0].astype(BF16)
    w_gate = jnp.pad(w_in[:, :, c0:c1], ((0, 0), (0, 0), (0, LANES - fox_heads))).astype(BF16)
    b_gate = jnp.pad(b_forget, ((0, 0), (0, LANES - fox_heads)))[:, None, :]
    w_diff = w_in[:, :, c1:].astype(BF16)
    assert fox_w == diff_w
    w_o_b = w_o.astype(BF16)
    w_up_b = w_up.astype(BF16)
    w_down_b = w_down.astype(BF16)
    dqn = jnp.tile(diff_q_norm, (1, 2))[:, None, :]
    dkn = jnp.tile(diff_k_norm, (1, 2))[:, None, :]

    pt_flat = page_table.reshape(-1)
    cache_fk = cache_fox_k.reshape(depth, n_phys, page * fox_heads, fox_dim)
    cache_fv = cache_fox_v.reshape(depth, n_phys, page * fox_heads, fox_dim)
    cache_dv = cache_diff_v.reshape(depth, n_phys, page * diff_heads, diff_dim)
    cache_dk = jnp.transpose(cache_diff_k, (0, 1, 3, 4, 5, 2)).reshape(depth, n_phys, diff_w, page)
    cache_lf = jnp.transpose(cache_fox_logf, (0, 1, 3, 2))
    group = _largest_divisor(n_pages, PAGES_PER_STEP)

    fox_scale = fox_dim ** -0.5
    diff_scale = dk ** -0.5

    def trunk(x, pos, decode):
        m = x.shape[0]
        n_seq = dec_batch if decode else batch
        t = m // n_seq
        tm = m if decode else ROW_TILE
        cos, sin = _rope_tables(pos, dk)
        if decode:
            cos, sin = jnp.tile(cos, (n_seq, 1)), jnp.tile(sin, (n_seq, 1))
        q_unit = 1.0 if decode else LOG2E
        news = []
        fox_stacked, diff_stacked = (), ()
        for l in range(depth):
            lam_init = 0.8 - 0.6 * math.exp(-0.3 * l)
            lam_vecs = (lambda_q1[l][None], lambda_k1[l][None], lambda_q2[l][None],
                        lambda_k2[l][None])
            place = dict(final=not decode, batch=n_seq, depth=depth, layer=l)
            fox = _proj("fox", x, norm_attn[l][None], w_fox, fox_q_norm[l][None],
                        fox_k_norm[l][None], (w_gate, b_gate[l]), tm=tm,
                        scale=fox_scale * q_unit, prev=fox_stacked, **place)
            dif = _proj("diff", x, norm_attn[l][None], w_diff, dqn[l], dkn[l], (cos, sin),
                        tm=tm, scale=diff_scale * q_unit, prev=diff_stacked, **place)
            logf = fox[3]
            if decode:
                fq, fk, fv = fox[:3]
                dq, dkk, dv = dif[:3]
                logf_h = logf[:, :fox_heads]
                pad = lambda a: jnp.pad(a.reshape(n_seq, t, -1), ((0, 0), (0, page - t), (0, 0)))
                rsum = _pastsum(cache_lf, pt_flat, l, n_seq)
                lw = jnp.repeat(logf_h.reshape(n_seq, t, fox_heads).transpose(0, 2, 1), t, axis=1)
                fox_o = _decode(fq, cache_fk, cache_fv, pad(fk), pad(fv), pt_flat, l, n_seq,
                                (rsum, lw), diff=False, group=group)
                diff_o = _decode(dq, cache_dk, cache_dv, pad(dkk), pad(dv), pt_flat, l, n_seq,
                                 lam_vecs + (diff_subln[l][None],), diff=True, group=group,
                                 lam_init=lam_init)
                news.append((fk, fv, logf_h, dkk, dv))
            else:
                fq, fkb, fvt = fox[:3]
                dq, dkb, dvt = dif[:3]
                fox_stacked, diff_stacked = tuple(fox[4:]), tuple(dif[3:])
                qa, ka = _cumaug(logf, n_seq, fox_heads, tc=ROW_TILE)
                fox_o = _flash(fq, fkb, fvt, n_seq, (qa, ka), diff=False, tq=ROW_TILE)
                diff_o = _flash(dq, dkb, dvt, n_seq, lam_vecs + (diff_subln[l][:, None],),
                                diff=True, tq=ROW_TILE, lam_init=lam_init)
            x = _oproj(x, fox_o, diff_o, w_o_b, l, tm=tm)
            x = _mlp(x, norm_mlp[l][None], w_up_b, w_down_b, l, tm=tm, tf=MLP_FF_TILE)
        lead = (depth, n_seq, t)
        if decode:
            stack = lambda i, shape: jnp.stack([n[i] for n in news]).reshape(lead + shape)
            new_cache = (stack(0, (fox_heads, fox_dim)), stack(1, (fox_heads, fox_dim)),
                         stack(2, (fox_heads,)), stack(3, (diff_heads, 2, dk)),
                         stack(4, (diff_heads, diff_dim)))
        else:
            fk, fv, logf_t = fox_stacked
            dk_t, dv = diff_stacked
            new_cache = (fk.reshape(lead + (fox_heads, fox_dim)),
                         fv.reshape(lead + (fox_heads, fox_dim)),
                         jnp.transpose(logf_t, (0, 1, 3, 2)),
                         jnp.transpose(dk_t.reshape(depth, n_seq, diff_heads, 2, dk, t),
                                       (0, 1, 5, 2, 3, 4)),
                         dv.reshape(lead + (diff_heads, diff_dim)))
        return (x.reshape(n_seq, t, d_model),) + new_cache

    pos_prompt = jnp.arange(seq, dtype=jnp.int32)
    pos_sample = past_len + jnp.arange(dec_seq, dtype=jnp.int32)
    yp = trunk(x_prompt.reshape(batch * seq, d_model), pos_prompt, False)
    ys = trunk(x_sample.reshape(dec_batch * dec_seq, d_model), pos_sample, True)
    return (yp[0], ys[0]) + yp[1:] + ys[1:]
```

```python
import functools
import math

import jax
import jax.numpy as jnp
from jax import lax
from jax.experimental import pallas as pl
from jax.experimental.pallas import tpu as pltpu

F32 = jnp.float32
BF16 = jnp.bfloat16
EPS = 1e-6
ROPE_THETA = 10000.0
NEG_INF = float("-inf")
MIB = 1024 * 1024

LANES = 128
MXU_WIDTH = 256
ROW_TILE = 512
MLP_FF_TILE = 1024
PAGES_PER_STEP = 16
SUM_ROWS = 16
LOG2E = math.log2(math.e)


def _params(sem, vmem_mib):
    return pltpu.CompilerParams(dimension_semantics=sem, vmem_limit_bytes=vmem_mib * MIB)


def _split3(x):
    hi = x.astype(BF16)
    r1 = x - hi.astype(F32)
    mid = r1.astype(BF16)
    lo = (r1 - mid.astype(F32)).astype(BF16)
    return hi, mid, lo


def _dot(a, b):
    return jnp.dot(a, b, preferred_element_type=F32)


def _dot_nt(a, b):
    return lax.dot_general(a, b, (((1,), (1,)), ((), ())), preferred_element_type=F32)


def _dot3(x, w):
    hi, mid, lo = _split3(x)
    return _dot(hi, w) + _dot(mid, w) + _dot(lo, w)


def _dot3_left(w, x):
    hi, mid, lo = _split3(x)
    return _dot(w, hi) + _dot(w, mid) + _dot(w, lo)


def _log_sigmoid(x):
    return jnp.minimum(x, 0.0) - jnp.log1p(jnp.exp(-jnp.abs(x)))


def _rms_rows(x, g):
    ms = jnp.mean(x * x, axis=-1, keepdims=True)
    return x * lax.rsqrt(ms + EPS) * g


def _proj_kernel(*refs, kind, n_heads, scale, final, layer, n_prev):
    x_ref, g_ref, w_ref, qn_ref, kn_ref, e0_ref, e1_ref = refs[:7]
    outs = refs[7 + n_prev:]
    fox = kind == "fox"
    if final:
        q_out, kb_out, vt_out = outs[:3]
        stacked = list(outs[4:] if fox else outs[3:])
        if n_prev == 0:
            for i, ref in enumerate(stacked):
                for other in range(ref.shape[0]):
                    if other != layer:
                        ref[other] = jnp.zeros(ref.shape[1:], F32)
                stacked[i] = ref.at[layer]
        if fox:
            logf_out = outs[3]
            k_fin, v_fin, logft_out = stacked
        else:
            k_fin, v_fin = stacked
    else:
        q_out, k_out, v_out = outs[:3]
        if fox:
            logf_out = outs[3]
    tm = x_ref.shape[0]
    w = n_heads * LANES

    xn = _rms_rows(x_ref[...], g_ref[...]).astype(BF16)
    if fox:
        logf = _log_sigmoid(_dot_nt(xn, e0_ref[...]) + e1_ref[...])
        logf_out[...] = logf
        if final:
            logft_out[...] = logf.T[0:n_heads, :]

    def heads(segment):
        acc = _dot_nt(xn, w_ref[segment * w:(segment + 1) * w, :])
        return [acc[:, h * LANES:(h + 1) * LANES] for h in range(n_heads)]

    if fox:
        def normed(blk, gain):
            return _rms_rows(blk, gain)
    else:
        lane = lax.broadcasted_iota(jnp.int32, (tm, LANES), 1)
        lane_lo = lane < (LANES // 2)
        rot_up = (lane % (LANES // 2)) < (LANES // 4)
        cos = e0_ref[...]
        sin = e1_ref[...]

        def normed(blk, gain):
            sq = blk * blk
            s_lo = jnp.sum(jnp.where(lane_lo, sq, 0.0), axis=-1, keepdims=True)
            s_hi = jnp.sum(jnp.where(lane_lo, 0.0, sq), axis=-1, keepdims=True)
            ms = jnp.where(lane_lo, s_lo, s_hi) * (2.0 / LANES)
            y = blk * lax.rsqrt(ms + EPS) * gain
            up = pltpu.roll(y, LANES - LANES // 4, axis=1)
            dn = pltpu.roll(y, LANES // 4, axis=1)
            return y * cos + jnp.where(rot_up, up, dn) * sin

    def head_lanes(h):
        return slice(h * LANES, (h + 1) * LANES)

    def token_head_rows(h):
        return pl.ds(h, tm, stride=n_heads)

    gain = qn_ref[...]
    for h, blk in enumerate(heads(0)):
        q_out[:, head_lanes(h)] = (normed(blk, gain) * scale).astype(q_out.dtype)

    gain = kn_ref[...]
    for h, blk in enumerate(heads(1)):
        y = normed(blk, gain)
        if not final:
            k_out[:, head_lanes(h)] = y
            continue
        kb_out[:, head_lanes(h)] = y.astype(BF16)
        if fox:
            k_fin[token_head_rows(h), :] = y
        else:
            k_fin[head_lanes(h), :] = y.T

    acc = _dot_nt(xn, w_ref[2 * w:, :])
    if final:
        vt_out[...] = acc.T.astype(BF16).reshape(vt_out.shape)
        for h in range(n_heads):
            v_fin[token_head_rows(h), :] = acc[:, head_lanes(h)]
    else:
        v_out[...] = acc


def _proj(kind, x, g, w3, qn, kn, extra, *, tm, scale, final, w=None, batch=1, depth=1,
          layer=0, prev=()):
    m, d = x.shape
    w = w3.shape[1] // 3 if w is None else w
    n_heads = w // LANES
    fox = kind == "fox"
    nb = m // batch // tm
    row = lambda i: (i, 0)
    const = lambda i: (0, 0)
    of_layer = lambda i: (layer, 0, 0)
    in_specs = [
        pl.BlockSpec((tm, d), row),
        pl.BlockSpec((1, d), const),
        pl.BlockSpec((None, 3 * w, d), of_layer, pipeline_mode=pl.Buffered(1)),
        pl.BlockSpec((1, LANES), const),
        pl.BlockSpec((1, LANES), const),
    ]
    if fox:
        in_specs += [pl.BlockSpec((None, LANES, d), lambda i: (layer, 3 * w // LANES, 0)),
                     pl.BlockSpec((1, LANES), const)]
    else:
        tab = lambda i: (i % nb, 0)
        in_specs += [pl.BlockSpec((tm, LANES), tab), pl.BlockSpec((tm, LANES), tab)]
    in_specs += [pl.BlockSpec(memory_space=pl.ANY) for _ in prev]

    rows = (jax.ShapeDtypeStruct((m, w), F32), pl.BlockSpec((tm, w), row))
    rows_bf16 = (jax.ShapeDtypeStruct((m, w), BF16), pl.BlockSpec((tm, w), row))
    logf = (jax.ShapeDtypeStruct((m, LANES), F32), pl.BlockSpec((tm, LANES), row))
    if final:
        seq = m // batch
        slabs, slab0 = (None, layer) if prev else (depth, 0)
        token_head = (jax.ShapeDtypeStruct((depth, m * n_heads, LANES), F32),
                      pl.BlockSpec((slabs, tm * n_heads, LANES), lambda i: (slab0, i, 0)))
        token_minor = lambda r: (jax.ShapeDtypeStruct((depth, batch, r, seq), F32),
                                 pl.BlockSpec((slabs, None, r, tm),
                                              lambda i: (slab0, i // nb, 0, i % nb)))
        vt = (jax.ShapeDtypeStruct((n_heads, m // tm, LANES, tm), BF16),
              pl.BlockSpec((n_heads, 1, LANES, tm), lambda i: (0, i, 0, 0)))
        outs = [rows_bf16, rows_bf16, vt]
        if fox:
            outs += [logf, token_head, token_head, token_minor(n_heads)]
        else:
            outs += [token_minor(w), token_head]
        first_stacked = len(outs) - (3 if fox else 2)
        aliases = {7 + a: first_stacked + a for a in range(len(prev))}
    else:
        outs = [rows, rows, rows] + ([logf] if fox else [])
        aliases = {}
    return pl.pallas_call(
        functools.partial(_proj_kernel, kind=kind, n_heads=n_heads, scale=scale, final=final,
                          layer=layer, n_prev=len(prev)),
        grid=(m // tm,),
        in_specs=in_specs,
        out_specs=[o[1] for o in outs],
        out_shape=[o[0] for o in outs],
        input_output_aliases=aliases,
        compiler_params=_params(("parallel",), 60),
    )(x, g, w3, qn, kn, *extra, *prev)


def _cumaug_kernel(logf_ref, tri_ref, selq_ref, selk_ref, oneq_ref, onek_ref,
                   qa_out, ka_out, carry_sc):
    @pl.when(pl.program_id(1) == 0)
    def _():
        carry_sc[...] = jnp.zeros_like(carry_sc)

    cum = _dot3_left(tri_ref[...], logf_ref[...]) + carry_sc[0:1, :]
    carry_sc[0:1, :] = cum[cum.shape[0] - 1:, :]
    pieces = jnp.concatenate(_split3(cum * LOG2E), axis=1)
    qa_out[...] = (_dot(pieces, selq_ref[...]) + oneq_ref[...]).astype(BF16)
    ka_out[...] = (_dot(pieces, selk_ref[...]) + onek_ref[...]).astype(BF16)


def _aug_constants(n_heads):
    w = n_heads * LANES
    r = jnp.arange(3 * LANES)
    c = jnp.arange(w)
    piece, head = r // LANES, r % LANES
    col_head, col_j = c // LANES, c % LANES
    same = (head[:, None] == col_head[None, :]) & (head[:, None] < n_heads)
    selq = jnp.where(same & (col_j[None, :] == piece[:, None]), 1.0, 0.0).astype(BF16)
    selk = jnp.where(same & (col_j[None, :] == piece[:, None] + 3), -1.0, 0.0).astype(BF16)
    oneq = jnp.where((col_j >= 3) & (col_j < 6), 1.0, 0.0).astype(F32)[None, :]
    onek = jnp.where(col_j < 3, 1.0, 0.0).astype(F32)[None, :]
    return selq, selk, oneq, onek


def _cumaug(logf, batch, n_heads, *, tc):
    m = logf.shape[0]
    t = m // batch
    nt = t // tc
    w = n_heads * LANES
    tri = (jnp.arange(tc)[:, None] >= jnp.arange(tc)[None, :]).astype(BF16)
    selq, selk, oneq, onek = _aug_constants(n_heads)
    const = lambda b, i: (0, 0)
    row = lambda b, i: (b * nt + i, 0)
    return pl.pallas_call(
        _cumaug_kernel,
        grid=(batch, nt),
        in_specs=[
            pl.BlockSpec((tc, LANES), row),
            pl.BlockSpec((tc, tc), const),
            pl.BlockSpec((3 * LANES, w), const),
            pl.BlockSpec((3 * LANES, w), const),
            pl.BlockSpec((1, w), const),
            pl.BlockSpec((1, w), const),
        ],
        out_specs=[pl.BlockSpec((tc, w), row), pl.BlockSpec((tc, w), row)],
        out_shape=[jax.ShapeDtypeStruct((m, w), BF16), jax.ShapeDtypeStruct((m, w), BF16)],
        scratch_shapes=[pltpu.VMEM((8, LANES), F32)],
        compiler_params=_params(("parallel", "arbitrary"), 32),
    )(logf, tri, selq, selk, oneq, onek)


def _lambda(lq1, lk1, lq2, lk2, lam_init):
    a = jnp.sum(lq1[...] * lk1[...], axis=-1, keepdims=True)
    b = jnp.sum(lq2[...] * lk2[...], axis=-1, keepdims=True)
    return jnp.exp(a) - jnp.exp(b) + lam_init


def _flash_kernel(*refs, diff, tq, lam_init):
    if diff:
        (q_ref, k_ref, vt_ref, lq1, lk1, lq2, lk2, sub_ref,
         o_ref, q_sc, m_sc, acc_sc, s_sc) = refs
    else:
        (q_ref, qa_ref, k_ref, ka_ref, vt_ref,
         o_ref, q_sc, m_sc, acc_sc, s_sc) = refs
    n_tiles = q_ref.shape[0] // tq
    cols = q_sc.shape[1]
    chunks = [slice(c * MXU_WIDTH, (c + 1) * MXU_WIDTH) for c in range(cols // MXU_WIDTH)]
    ones = jnp.ones((SUM_ROWS, tq), BF16)
    if diff:
        lam = _lambda(lq1, lk1, lq2, lk2, lam_init)

    def prepare(i):
        par = i
        rows = slice(i * tq, (i + 1) * tq)
        if diff:
            q = q_ref[rows, :]
            lane = lax.broadcasted_iota(jnp.int32, q.shape, 1)
            zero = jnp.zeros_like(q)
            q_sc[par, 0:tq, :] = jnp.where(lane < LANES // 2, q, zero)
            q_sc[par, tq:, :] = jnp.where(lane < LANES // 2, zero, q)
        else:
            q_sc[par, :, 0:LANES] = q_ref[rows, :]
            q_sc[par, :, LANES:] = qa_ref[rows, :]
        m_sc[par] = jnp.full(m_sc.shape[1:], NEG_INF, F32)
        acc_sc[par] = jnp.zeros(acc_sc.shape[1:], F32)

    def visible_keys(c, diagonal):
        return (c * MXU_WIDTH) % tq + MXU_WIDTH if diagonal else tq

    def scores(i, j, slot, diagonal=False):
        start = j * tq if isinstance(j, int) else pl.multiple_of(j * tq, tq)
        k = k_ref[pl.ds(start, tq), :]
        if not diff:
            k = jnp.concatenate([k, ka_ref[pl.ds(start, tq), :]], axis=1)
        for c, cs in enumerate(chunks):
            n = visible_keys(c, diagonal)
            s_sc[slot, 0:n, cs] = _dot_nt(k[0:n], q_sc[i, cs, :])

    def accumulate(i, j, slot, diagonal):
        par = i
        vt = jnp.concatenate([vt_ref[j], ones], axis=0)
        for c, cs in enumerate(chunks):
            n = visible_keys(c, diagonal)
            s = s_sc[slot, 0:n, cs]
            if diagonal:
                key = lax.broadcasted_iota(jnp.int32, s.shape, 0)
                qry = (lax.broadcasted_iota(jnp.int32, s.shape, 1) + c * MXU_WIDTH) % tq
                s = jnp.where(key <= qry, s, NEG_INF)
            m_prev = m_sc[par, :, cs]
            m_new = jnp.maximum(m_prev, jnp.max(s, axis=0, keepdims=True))
            alpha = jnp.exp2(m_prev - m_new)
            p = jnp.exp2(s - m_new).astype(BF16)
            acc_sc[par, :, cs] = alpha * acc_sc[par, :, cs] + _dot(vt[:, 0:n], p)
            m_sc[par, :, cs] = m_new

    def finalize(i):
        par = i
        out_t = acc_sc[par, 0:LANES, :] / acc_sc[par, LANES:LANES + 1, :]
        if diff:
            o = out_t[:, 0:tq] - lam * out_t[:, tq:]
            ms = jnp.mean(o * o, axis=0, keepdims=True)
            out_t = o * lax.rsqrt(ms + EPS) * sub_ref[...] * (1.0 - lam_init)
        o_ref[i * tq:(i + 1) * tq, :] = out_t.T.astype(o_ref.dtype)

    prepare(0)
    scores(0, 0, 0, diagonal=True)
    first = 0
    for i in range(n_tiles):
        a, b = first, 1 - first

        def pair(jj, carry, i=i, a=a, b=b):
            j = 2 * jj
            scores(i, j + 1, b)
            accumulate(i, j, a, False)
            scores(i, j + 2, a)
            accumulate(i, j + 1, b, False)
            return carry

        if i // 2:
            lax.fori_loop(0, i // 2, pair, 0)
        more = i + 1 < n_tiles
        if i % 2 == 0:
            if more:
                prepare(i + 1)
                scores(i + 1, 0, b)
            accumulate(i, i, a, True)
            first = b
        else:
            scores(i, i, b, diagonal=True)
            accumulate(i, i - 1, a, False)
            if more:
                prepare(i + 1)
                scores(i + 1, 0, a)
            accumulate(i, i, b, True)
            first = a
        finalize(i)


def _flash(q, k, vt, batch, extra, *, diff, tq, lam_init=0.0):
    m, w = q.shape
    n_heads = w // LANES
    t = m // batch
    nq = t // tq
    seq = pl.BlockSpec((t, LANES), lambda b, h: (b, h))
    vspec = pl.BlockSpec((None, nq, LANES, tq), lambda b, h: (h, b, 0, 0))
    if diff:
        vec = pl.BlockSpec((1, LANES // 2), lambda b, h: (0, 0))
        sub = pl.BlockSpec((LANES, 1), lambda b, h: (0, 0))
        in_specs = [seq, seq, vspec, vec, vec, vec, vec, sub]
        args = (q, k, vt) + tuple(extra)
        cols, qw = 2 * tq, LANES
    else:
        qa, ka = extra
        in_specs = [seq, seq, seq, seq, vspec]
        args = (q, qa, k, ka, vt)
        cols, qw = tq, 2 * LANES
    return pl.pallas_call(
        functools.partial(_flash_kernel, diff=diff, tq=tq, lam_init=lam_init),
        grid=(batch, n_heads),
        in_specs=in_specs,
        out_specs=seq,
        out_shape=jax.ShapeDtypeStruct((m, w), BF16),
        scratch_shapes=[
            pltpu.VMEM((nq, cols, qw), BF16),
            pltpu.VMEM((nq, 1, cols), F32),
            pltpu.VMEM((nq, LANES + SUM_ROWS, cols), F32),
            pltpu.VMEM((2, tq, cols), F32),
        ],
        compiler_params=_params(("parallel", "parallel"), 48),
    )(*args)


def _pastsum_kernel(pt_ref, logf_hbm, u_ref, ubig_ref, r_out, buf, sem, *, layer, n_pages):
    b = pl.program_id(0)

    def copy(p):
        page = pt_ref[b * n_pages + p]
        return pltpu.make_async_copy(logf_hbm.at[layer, page], buf.at[p], sem)

    def start(p, c):
        copy(p).start()
        return c

    def wait(p, c):
        copy(p).wait()
        return c

    lax.fori_loop(0, n_pages, start, 0)
    lax.fori_loop(0, n_pages, wait, 0)

    xt = buf[...].reshape(r_out.shape)
    within = _dot3(xt, u_ref[...])
    later = jnp.sum(_dot3_left(ubig_ref[...], xt), axis=1, keepdims=True)
    r_out[...] = within + later


def _pastsum(cache_logf_t, page_table_flat, layer, dec_batch):
    n_pages = page_table_flat.shape[0] // dec_batch
    n_heads, t = cache_logf_t.shape[2:]
    u = (jnp.arange(t)[:, None] > jnp.arange(t)[None, :]).astype(BF16)
    r = jnp.arange(n_pages * n_heads)
    ubig = ((r[None, :] % n_heads == r[:, None] % n_heads)
            & (r[None, :] // n_heads > r[:, None] // n_heads)).astype(BF16)
    rows = n_pages * n_heads
    const = lambda b, pt: (0, 0)
    grid_spec = pltpu.PrefetchScalarGridSpec(
        num_scalar_prefetch=1,
        grid=(dec_batch,),
        in_specs=[
            pl.BlockSpec(memory_space=pl.ANY),
            pl.BlockSpec((t, t), const),
            pl.BlockSpec((rows, rows), const),
        ],
        out_specs=pl.BlockSpec((None, rows, t), lambda b, pt: (b, 0, 0)),
        scratch_shapes=[pltpu.VMEM((n_pages, n_heads, t), F32), pltpu.SemaphoreType.DMA(())],
    )
    return pl.pallas_call(
        functools.partial(_pastsum_kernel, layer=layer, n_pages=n_pages),
        grid_spec=grid_spec,
        out_shape=jax.ShapeDtypeStruct((dec_batch, rows, t), F32),
        compiler_params=_params(("arbitrary",), 32),
    )(page_table_flat, cache_logf_t, u, ubig)


def _decode_kernel(*refs, diff, n_steps, group, n_heads, t_new, page, lam_init):
    refs = list(refs)
    pt_ref, q_ref = refs[0], refs[1]
    k_refs = refs[2:2 + group]
    v_refs = refs[2 + group:2 + 2 * group]
    rest = refs[2 + 2 * group:]
    if diff:
        (kn_ref, vn_ref, lq1, lk1, lq2, lk2, sub_ref,
         o_ref, q_sc, m_sc, l_sc, acc_sc) = rest
    else:
        (kn_ref, vn_ref, r_ref, lw_ref,
         o_ref, q_sc, m_sc, l_sc, acc_sc, qc_sc) = rest
    p = pl.program_id(1)
    rows = q_sc.shape[0]
    groups = 2 * n_heads if diff else n_heads
    group_w = q_sc.shape[1] // groups

    def q_index(shape):
        return lax.broadcasted_iota(jnp.int32, shape, 0) % t_new

    def page_rows(ref):
        heads = [ref[pl.ds(h, page, stride=n_heads), :] for h in range(n_heads)]
        return jnp.concatenate(heads, axis=1).astype(BF16)

    @pl.when(p == 0)
    def _():
        q = jnp.concatenate([q_ref[...]] * (rows // t_new), axis=0)
        r = lax.broadcasted_iota(jnp.int32, q.shape, 0)
        lane = lax.broadcasted_iota(jnp.int32, q.shape, 1)
        if diff:
            grp = ((r // t_new) % n_heads) * 2 + r // (t_new * n_heads)
        else:
            grp = r // t_new
        q_sc[...] = jnp.where(lane // group_w == grp, q, 0.0).astype(BF16)
        m_sc[...] = jnp.full_like(m_sc, NEG_INF)
        l_sc[...] = jnp.zeros_like(l_sc)
        acc_sc[...] = jnp.zeros_like(acc_sc)
        if not diff:
            lw = lw_ref[...]
            i_idx = lax.broadcasted_iota(jnp.int32, lw.shape, 1)
            qc = jnp.sum(jnp.where(i_idx <= q_index(lw.shape), lw, 0.0), axis=1, keepdims=True)
            qc_sc[...] = qc

    def update(s, values):
        m_prev = m_sc[...]
        m_new = jnp.maximum(m_prev, jnp.max(s, axis=-1, keepdims=True))
        alpha = jnp.exp(m_prev - m_new)
        pr = jnp.exp(s - m_new)
        l_sc[...] = alpha * l_sc[...] + jnp.sum(pr, axis=-1, keepdims=True)
        pr = pr.astype(BF16)
        pv = _dot(pr[:, 0:page], values[0])
        for g in range(1, len(values)):
            pv = pv + _dot(pr[:, g * page:(g + 1) * page], values[g])
        acc_sc[...] = alpha * acc_sc[...] + pv
        m_sc[...] = m_new

    @pl.when(p < n_steps)
    def _():
        parts = []
        for g in range(group):
            if diff:
                s = _dot(q_sc[...], k_refs[g][...].astype(BF16))
            else:
                s = _dot_nt(q_sc[...], page_rows(k_refs[g]))
                r8 = r_ref[g * n_heads:(g + 1) * n_heads, :]
                bias = jnp.concatenate(
                    [jnp.broadcast_to(r8[h:h + 1, :], (t_new, page)) for h in range(n_heads)],
                    axis=0)
                s = s + (qc_sc[...] + bias)
            parts.append(s)
        update(jnp.concatenate(parts, axis=1), [page_rows(v) for v in v_refs])

    @pl.when(p == n_steps)
    def _():
        s = _dot_nt(q_sc[...], kn_ref[...].astype(BF16))
        col = lax.broadcasted_iota(jnp.int32, s.shape, 1)
        qi = q_index(s.shape)
        if not diff:
            lw = lw_ref[...]
            bias = jnp.zeros(s.shape, F32)
            for i in range(t_new):
                bias = bias + jnp.where((qi >= i) & (col < i), lw[:, i:i + 1], 0.0)
            s = s + bias
        s = jnp.where(col <= qi, s, NEG_INF)
        update(s, [vn_ref[...].astype(BF16)])

        acc = acc_sc[...]
        inv_l = 1.0 / l_sc[...]
        if diff:
            lam = _lambda(lq1, lk1, lq2, lk2, lam_init)
            half = n_heads * t_new
        for h in range(n_heads):
            lanes = slice(h * LANES, (h + 1) * LANES)
            r0 = h * t_new
            o = acc[r0:r0 + t_new, lanes] * inv_l[r0:r0 + t_new, :]
            if diff:
                o2 = acc[half + r0:half + r0 + t_new, lanes] * inv_l[half + r0:half + r0 + t_new, :]
                o = _rms_rows(o - lam * o2, sub_ref[...]) * (1.0 - lam_init)
            o_ref[:, lanes] = o


def _decode(q, cache_k, cache_v, k_new, v_new, page_table_flat, layer, dec_batch, extra,
            *, diff, group, lam_init=0.0):
    w = q.shape[1]
    t_new = q.shape[0] // dec_batch
    page = k_new.shape[1]
    n_pages = page_table_flat.shape[0] // dec_batch
    n_steps = n_pages // group
    n_heads = w // LANES
    rows = (2 if diff else 1) * n_heads * t_new

    def page_map(g):
        def index(b, p, pt):
            return (layer, pt[b * n_pages + jnp.minimum(p, n_steps - 1) * group + g], 0, 0)
        return index

    per_b = lambda b, p, pt: (b, 0)
    per_b3 = lambda b, p, pt: (b, 0, 0)
    kblock = (None, None) + cache_k.shape[2:]
    vblock = (None, None) + cache_v.shape[2:]
    in_specs = [pl.BlockSpec((t_new, w), per_b)]
    in_specs += [pl.BlockSpec(kblock, page_map(g)) for g in range(group)]
    in_specs += [pl.BlockSpec(vblock, page_map(g)) for g in range(group)]
    in_specs += [pl.BlockSpec((None, page, w), per_b3), pl.BlockSpec((None, page, w), per_b3)]
    scratch = [
        pltpu.VMEM((rows, w), BF16),
        pltpu.VMEM((rows, 1), F32),
        pltpu.VMEM((rows, 1), F32),
        pltpu.VMEM((rows, w), F32),
    ]
    if diff:
        vec = pl.BlockSpec((1, LANES // 2), lambda b, p, pt: (0, 0))
        in_specs += [vec, vec, vec, vec, pl.BlockSpec((1, LANES), lambda b, p, pt: (0, 0))]
    else:
        in_specs += [
            pl.BlockSpec((None, group * n_heads, page),
                         lambda b, p, pt: (b, jnp.minimum(p, n_steps - 1), 0)),
            pl.BlockSpec((None, rows, t_new), per_b3),
        ]
        scratch.append(pltpu.VMEM((rows, 1), F32))
    grid_spec = pltpu.PrefetchScalarGridSpec(
        num_scalar_prefetch=1,
        grid=(dec_batch, n_steps + 1),
        in_specs=in_specs,
        out_specs=pl.BlockSpec((t_new, w), per_b),
        scratch_shapes=scratch,
    )
    return pl.pallas_call(
        functools.partial(_decode_kernel, diff=diff, n_steps=n_steps, group=group,
                          n_heads=n_heads, t_new=t_new, page=page, lam_init=lam_init),
        grid_spec=grid_spec,
        out_shape=jax.ShapeDtypeStruct((dec_batch * t_new, w), F32),
        compiler_params=_params(("parallel", "arbitrary"), 48),
    )(page_table_flat, q, *([cache_k] * group), *([cache_v] * group), k_new, v_new, *extra)


def _oproj_kernel(x_ref, a_ref, b_ref, wa_ref, wb_ref, o_ref):
    o_ref[...] = (x_ref[...] + _dot(a_ref[...].astype(BF16), wa_ref[...])
                  + _dot(b_ref[...].astype(BF16), wb_ref[...]))


def _oproj(x, a, b, w_o, layer, *, tm):
    m, d = x.shape
    w = a.shape[1]
    row = lambda i: (i, 0)
    return pl.pallas_call(
        _oproj_kernel,
        grid=(m // tm,),
        in_specs=[
            pl.BlockSpec((tm, d), row),
            pl.BlockSpec((tm, w), row),
            pl.BlockSpec((tm, w), row),
            pl.BlockSpec((None, w, d), lambda i: (layer, 0, 0)),
            pl.BlockSpec((None, w, d), lambda i: (layer, 1, 0)),
        ],
        out_specs=pl.BlockSpec((tm, d), row),
        out_shape=jax.ShapeDtypeStruct((m, d), F32),
        compiler_params=_params(("parallel",), 48),
    )(x, a, b, w_o, w_o)


def _mlp_kernel(x_ref, g_ref, wu_ref, wd_ref, o_ref, xn_sc, acc_sc):
    j = pl.program_id(1)

    @pl.when(j == 0)
    def _():
        x = x_ref[...]
        xn_sc[...] = _rms_rows(x, g_ref[...]).astype(BF16)
        acc_sc[...] = x

    h = jnp.maximum(_dot(xn_sc[...], wu_ref[...]), 0.0)
    acc_sc[...] += _dot((h * h).astype(BF16), wd_ref[...])

    @pl.when(j == pl.num_programs(1) - 1)
    def _():
        o_ref[...] = acc_sc[...]


def _mlp(x, g, wu, wd, layer, *, tm, tf):
    m, d = x.shape
    f = wu.shape[2]
    row = lambda i, j: (i, 0)
    return pl.pallas_call(
        _mlp_kernel,
        grid=(m // tm, f // tf),
        in_specs=[
            pl.BlockSpec((tm, d), row),
            pl.BlockSpec((1, d), lambda i, j: (0, 0)),
            pl.BlockSpec((None, d, tf), lambda i, j: (layer, 0, j)),
            pl.BlockSpec((None, tf, d), lambda i, j: (layer, j, 0)),
        ],
        out_specs=pl.BlockSpec((tm, d), row),
        out_shape=jax.ShapeDtypeStruct((m, d), F32),
        scratch_shapes=[pltpu.VMEM((tm, d), BF16), pltpu.VMEM((tm, d), F32)],
        compiler_params=_params(("parallel", "arbitrary"), 56),
    )(x, g, wu, wd)


def _rope_tables(pos, dk):
    inv_freq = ROPE_THETA ** (-jnp.arange(0, dk, 2, dtype=F32) / dk)
    ang = pos.astype(F32)[:, None] * inv_freq[None, :]
    cos, sin = jnp.cos(ang), jnp.sin(ang)
    reps = LANES // dk
    cos = jnp.tile(jnp.concatenate([cos, cos], axis=-1), (1, reps))
    sin = jnp.tile(jnp.concatenate([-sin, sin], axis=-1), (1, reps))
    return cos, sin


def _largest_divisor(n, cap):
    return max(d for d in range(1, cap + 1) if n % d == 0)


def kernel(x_prompt, x_sample, cache_fox_k, cache_fox_v, cache_fox_logf, cache_diff_k, cache_diff_v, page_table, norm_attn, w_in, b_forget, fox_q_norm, fox_k_norm, diff_q_norm, diff_k_norm, lambda_q1, lambda_k1, lambda_q2, lambda_k2, diff_subln, w_o, norm_mlp, w_up, w_down):
    depth = w_in.shape[0]
    batch, seq, d_model = x_prompt.shape
    dec_batch, dec_seq, _ = x_sample.shape
    _, n_phys, page, fox_heads, fox_dim = cache_fox_k.shape
    diff_heads, diff_dim = cache_diff_v.shape[3], cache_diff_v.shape[4]
    dk = cache_diff_k.shape[5]
    fox_w = fox_heads * fox_dim
    diff_w = diff_heads * diff_dim
    n_pages = page_table.shape[1]
    past_len = n_pages * page
    assert fox_dim == LANES and diff_dim == LANES and 2 * dk == LANES
    assert seq % ROW_TILE == 0

    c1 = 3 * fox_w + fox_heads
    w_in_b = jnp.transpose(w_in, (0, 2, 1)).astype(BF16)
    b_gate = jnp.pad(b_forget, ((0, 0), (0, LANES - fox_heads)))[:, None, :]
    w_diff = w_in_b[:, c1:, :]
    assert fox_w == diff_w
    w_o_b = w_o.astype(BF16)
    w_up_b = w_up.astype(BF16)
    w_down_b = w_down.astype(BF16)
    dqn = jnp.tile(diff_q_norm, (1, 2))[:, None, :]
    dkn = jnp.tile(diff_k_norm, (1, 2))[:, None, :]

    pt_flat = page_table.reshape(-1)
    cache_fk = cache_fox_k.reshape(depth, n_phys, page * fox_heads, fox_dim)
    cache_fv = cache_fox_v.reshape(depth, n_phys, page * fox_heads, fox_dim)
    cache_dv = cache_diff_v.reshape(depth, n_phys, page * diff_heads, diff_dim)
    cache_dk = jnp.transpose(cache_diff_k, (0, 1, 3, 4, 5, 2)).reshape(depth, n_phys, diff_w, page)
    cache_lf = jnp.transpose(cache_fox_logf, (0, 1, 3, 2))
    group = _largest_divisor(n_pages, PAGES_PER_STEP)

    fox_scale = fox_dim ** -0.5
    diff_scale = dk ** -0.5

    def trunk(x, pos, decode):
        m = x.shape[0]
        n_seq = dec_batch if decode else batch
        t = m // n_seq
        tm = m if decode else ROW_TILE
        cos, sin = _rope_tables(pos, dk)
        if decode:
            cos, sin = jnp.tile(cos, (n_seq, 1)), jnp.tile(sin, (n_seq, 1))
        q_unit = 1.0 if decode else LOG2E
        news = []
        fox_stacked, diff_stacked = (), ()
        for l in range(depth):
            lam_init = 0.8 - 0.6 * math.exp(-0.3 * l)
            lam_vecs = (lambda_q1[l][None], lambda_k1[l][None], lambda_q2[l][None],
                        lambda_k2[l][None])
            place = dict(final=not decode, batch=n_seq, depth=depth, layer=l)
            fox = _proj("fox", x, norm_attn[l][None], w_in_b, fox_q_norm[l][None],
                        fox_k_norm[l][None], (w_in_b, b_gate[l]), tm=tm, w=fox_w,
                        scale=fox_scale * q_unit, prev=fox_stacked, **place)
            dif = _proj("diff", x, norm_attn[l][None], w_diff, dqn[l], dkn[l], (cos, sin),
                        tm=tm, scale=diff_scale * q_unit, prev=diff_stacked, **place)
            logf = fox[3]
            if decode:
                fq, fk, fv = fox[:3]
                dq, dkk, dv = dif[:3]
                logf_h = logf[:, :fox_heads]
                pad = lambda a: jnp.pad(a.reshape(n_seq, t, -1), ((0, 0), (0, page - t), (0, 0)))
                rsum = _pastsum(cache_lf, pt_flat, l, n_seq)
                lw = jnp.repeat(logf_h.reshape(n_seq, t, fox_heads).transpose(0, 2, 1), t, axis=1)
                fox_o = _decode(fq, cache_fk, cache_fv, pad(fk), pad(fv), pt_flat, l, n_seq,
                                (rsum, lw), diff=False, group=group)
                diff_o = _decode(dq, cache_dk, cache_dv, pad(dkk), pad(dv), pt_flat, l, n_seq,
                                 lam_vecs + (diff_subln[l][None],), diff=True, group=group,
                                 lam_init=lam_init)
                news.append((fk, fv, logf_h, dkk, dv))
            else:
                fq, fkb, fvt = fox[:3]
                dq, dkb, dvt = dif[:3]
                fox_stacked, diff_stacked = tuple(fox[4:]), tuple(dif[3:])
                qa, ka = _cumaug(logf, n_seq, fox_heads, tc=ROW_TILE)
                fox_o = _flash(fq, fkb, fvt, n_seq, (qa, ka), diff=False, tq=ROW_TILE)
                diff_o = _flash(dq, dkb, dvt, n_seq, lam_vecs + (diff_subln[l][:, None],),
                                diff=True, tq=ROW_TILE, lam_init=lam_init)
            x = _oproj(x, fox_o, diff_o, w_o_b, l, tm=tm)
            x = _mlp(x, norm_mlp[l][None], w_up_b, w_down_b, l, tm=tm, tf=MLP_FF_TILE)
        lead = (depth, n_seq, t)
        if decode:
            stack = lambda i, shape: jnp.stack([n[i] for n in news]).reshape(lead + shape)
            new_cache = (stack(0, (fox_heads, fox_dim)), stack(1, (fox_heads, fox_dim)),
                         stack(2, (fox_heads,)), stack(3, (diff_heads, 2, dk)),
                         stack(4, (diff_heads, diff_dim)))
        else:
            fk, fv, logf_t = fox_stacked
            dk_t, dv = diff_stacked
            new_cache = (fk.reshape(lead + (fox_heads, fox_dim)),
                         fv.reshape(lead + (fox_heads, fox_dim)),
                         jnp.transpose(logf_t, (0, 1, 3, 2)),
                         jnp.transpose(dk_t.reshape(depth, n_seq, diff_heads, 2, dk, t),
                                       (0, 1, 5, 2, 3, 4)),
                         dv.reshape(lead + (diff_heads, diff_dim)))
        return (x.reshape(n_seq, t, d_model),) + new_cache

    pos_prompt = jnp.arange(seq, dtype=jnp.int32)
    pos_sample = past_len + jnp.arange(dec_seq, dtype=jnp.int32)
    yp = trunk(x_prompt.reshape(batch * seq, d_model), pos_prompt, False)
    ys = trunk(x_sample.reshape(dec_batch * dec_seq, d_model), pos_sample, True)
    return (yp[0], ys[0]) + yp[1:] + ys[1:]
```

```python
import functools
import math

import jax
import jax.numpy as jnp
from jax import lax
from jax.experimental import pallas as pl
from jax.experimental.pallas import tpu as pltpu

F32 = jnp.float32
BF16 = jnp.bfloat16
EPS = 1e-6
ROPE_THETA = 10000.0
NEG_INF = float("-inf")
MIB = 1024 * 1024

LANES = 128
MXU_WIDTH = 256
ROW_TILE = 512
MLP_FF_TILE = 1024
PAGES_PER_STEP = 16
SUM_ROWS = 16
LOG2E = math.log2(math.e)


def _params(sem, vmem_mib):
    return pltpu.CompilerParams(dimension_semantics=sem, vmem_limit_bytes=vmem_mib * MIB)


def _split3(x):
    hi = x.astype(BF16)
    r1 = x - hi.astype(F32)
    mid = r1.astype(BF16)
    lo = (r1 - mid.astype(F32)).astype(BF16)
    return hi, mid, lo


def _dot(a, b):
    return jnp.dot(a, b, preferred_element_type=F32)


def _dot_nt(a, b):
    return lax.dot_general(a, b, (((1,), (1,)), ((), ())), preferred_element_type=F32)


def _dot3(x, w):
    hi, mid, lo = _split3(x)
    return _dot(hi, w) + _dot(mid, w) + _dot(lo, w)


def _dot3_left(w, x):
    hi, mid, lo = _split3(x)
    return _dot(w, hi) + _dot(w, mid) + _dot(w, lo)


def _log_sigmoid(x):
    return jnp.minimum(x, 0.0) - jnp.log1p(jnp.exp(-jnp.abs(x)))


def _rms_rows(x, g):
    ms = jnp.mean(x * x, axis=-1, keepdims=True)
    return x * lax.rsqrt(ms + EPS) * g


def _proj_kernel(*refs, kind, n_heads, scale, final, layer, n_prev):
    x_ref, g_ref, w_ref, qn_ref, kn_ref, e0_ref, e1_ref = refs[:7]
    outs = refs[7 + n_prev:]
    fox = kind == "fox"
    if final:
        q_out, kb_out, vt_out = outs[:3]
        stacked = list(outs[4:] if fox else outs[3:])
        if n_prev == 0:
            for i, ref in enumerate(stacked):
                for other in range(ref.shape[0]):
                    if other != layer:
                        ref[other] = jnp.zeros(ref.shape[1:], F32)
                stacked[i] = ref.at[layer]
        if fox:
            logf_out = outs[3]
            k_fin, v_fin, logft_out = stacked
        else:
            k_fin, v_fin = stacked
    else:
        q_out, k_out, v_out = outs[:3]
        if fox:
            logf_out = outs[3]
    tm = x_ref.shape[0]
    w = n_heads * LANES

    xn = _rms_rows(x_ref[...], g_ref[...]).astype(BF16)
    if fox:
        logf = _log_sigmoid(_dot_nt(xn, e0_ref[...]) + e1_ref[...])
        logf_out[...] = logf
        if final:
            logft_out[...] = logf.T[0:n_heads, :]

    def heads(segment):
        acc = _dot_nt(xn, w_ref[segment * w:(segment + 1) * w, :])
        return [acc[:, h * LANES:(h + 1) * LANES] for h in range(n_heads)]

    if fox:
        def normed(blk, gain):
            return _rms_rows(blk, gain)
    else:
        lane = lax.broadcasted_iota(jnp.int32, (tm, LANES), 1)
        lane_lo = lane < (LANES // 2)
        rot_up = (lane % (LANES // 2)) < (LANES // 4)
        cos = e0_ref[...]
        sin = e1_ref[...]

        def normed(blk, gain):
            sq = blk * blk
            s_lo = jnp.sum(jnp.where(lane_lo, sq, 0.0), axis=-1, keepdims=True)
            s_hi = jnp.sum(jnp.where(lane_lo, 0.0, sq), axis=-1, keepdims=True)
            ms = jnp.where(lane_lo, s_lo, s_hi) * (2.0 / LANES)
            y = blk * lax.rsqrt(ms + EPS) * gain
            up = pltpu.roll(y, LANES - LANES // 4, axis=1)
            dn = pltpu.roll(y, LANES // 4, axis=1)
            return y * cos + jnp.where(rot_up, up, dn) * sin

    def head_lanes(h):
        return slice(h * LANES, (h + 1) * LANES)

    def token_head_rows(h):
        return pl.ds(h, tm, stride=n_heads)

    gain = qn_ref[...]
    for h, blk in enumerate(heads(0)):
        q_out[:, head_lanes(h)] = (normed(blk, gain) * scale).astype(q_out.dtype)

    gain = kn_ref[...]
    for h, blk in enumerate(heads(1)):
        y = normed(blk, gain)
        if not final:
            k_out[:, head_lanes(h)] = y
            continue
        kb_out[:, head_lanes(h)] = y.astype(BF16)
        if fox:
            k_fin[token_head_rows(h), :] = y
        else:
            k_fin[head_lanes(h), :] = y.T

    acc = _dot_nt(xn, w_ref[2 * w:, :])
    if final:
        vt_out[...] = acc.T.astype(BF16).reshape(vt_out.shape)
        for h in range(n_heads):
            v_fin[token_head_rows(h), :] = acc[:, head_lanes(h)]
    else:
        v_out[...] = acc


def _proj(kind, x, g, w3, qn, kn, extra, *, tm, scale, final, w=None, batch=1, depth=1,
          layer=0, prev=()):
    m, d = x.shape
    w = w3.shape[1] // 3 if w is None else w
    n_heads = w // LANES
    fox = kind == "fox"
    nb = m // batch // tm
    row = lambda i: (i, 0)
    const = lambda i: (0, 0)
    of_layer = lambda i: (layer, 0, 0)
    in_specs = [
        pl.BlockSpec((tm, d), row),
        pl.BlockSpec((1, d), const),
        pl.BlockSpec((None, 3 * w, d), of_layer, pipeline_mode=pl.Buffered(1)),
        pl.BlockSpec((1, LANES), const),
        pl.BlockSpec((1, LANES), const),
    ]
    if fox:
        in_specs += [pl.BlockSpec((None, LANES, d), lambda i: (layer, 3 * w // LANES, 0)),
                     pl.BlockSpec((1, LANES), const)]
    else:
        tab = lambda i: (i % nb, 0)
        in_specs += [pl.BlockSpec((tm, LANES), tab), pl.BlockSpec((tm, LANES), tab)]
    in_specs += [pl.BlockSpec(memory_space=pl.ANY) for _ in prev]

    rows = (jax.ShapeDtypeStruct((m, w), F32), pl.BlockSpec((tm, w), row))
    rows_bf16 = (jax.ShapeDtypeStruct((m, w), BF16), pl.BlockSpec((tm, w), row))
    logf = (jax.ShapeDtypeStruct((m, LANES), F32), pl.BlockSpec((tm, LANES), row))
    if final:
        seq = m // batch
        slabs, slab0 = (None, layer) if prev else (depth, 0)
        token_head = (jax.ShapeDtypeStruct((depth, m * n_heads, LANES), F32),
                      pl.BlockSpec((slabs, tm * n_heads, LANES), lambda i: (slab0, i, 0)))
        token_minor = lambda r: (jax.ShapeDtypeStruct((depth, batch, r, seq), F32),
                                 pl.BlockSpec((slabs, None, r, tm),
                                              lambda i: (slab0, i // nb, 0, i % nb)))
        vt = (jax.ShapeDtypeStruct((n_heads, m // tm, LANES, tm), BF16),
              pl.BlockSpec((n_heads, 1, LANES, tm), lambda i: (0, i, 0, 0)))
        outs = [rows_bf16, rows_bf16, vt]
        if fox:
            outs += [logf, token_head, token_head, token_minor(n_heads)]
        else:
            outs += [token_minor(w), token_head]
        first_stacked = len(outs) - (3 if fox else 2)
        aliases = {7 + a: first_stacked + a for a in range(len(prev))}
    else:
        outs = [rows, rows, rows] + ([logf] if fox else [])
        aliases = {}
    return pl.pallas_call(
        functools.partial(_proj_kernel, kind=kind, n_heads=n_heads, scale=scale, final=final,
                          layer=layer, n_prev=len(prev)),
        grid=(m // tm,),
        in_specs=in_specs,
        out_specs=[o[1] for o in outs],
        out_shape=[o[0] for o in outs],
        input_output_aliases=aliases,
        compiler_params=_params(("parallel",), 60),
    )(x, g, w3, qn, kn, *extra, *prev)


def _cumaug_kernel(logf_ref, tri_ref, selq_ref, selk_ref, oneq_ref, onek_ref,
                   qa_out, ka_out, carry_sc):
    @pl.when(pl.program_id(1) == 0)
    def _():
        carry_sc[...] = jnp.zeros_like(carry_sc)

    cum = _dot3_left(tri_ref[...], logf_ref[...]) + carry_sc[0:1, :]
    carry_sc[0:1, :] = cum[cum.shape[0] - 1:, :]
    pieces = jnp.concatenate(_split3(cum * LOG2E), axis=1)
    qa_out[...] = (_dot(pieces, selq_ref[...]) + oneq_ref[...]).astype(BF16)
    ka_out[...] = (_dot(pieces, selk_ref[...]) + onek_ref[...]).astype(BF16)


def _aug_constants(n_heads):
    w = n_heads * LANES
    r = jnp.arange(3 * LANES)
    c = jnp.arange(w)
    piece, head = r // LANES, r % LANES
    col_head, col_j = c // LANES, c % LANES
    same = (head[:, None] == col_head[None, :]) & (head[:, None] < n_heads)
    selq = jnp.where(same & (col_j[None, :] == piece[:, None]), 1.0, 0.0).astype(BF16)
    selk = jnp.where(same & (col_j[None, :] == piece[:, None] + 3), -1.0, 0.0).astype(BF16)
    oneq = jnp.where((col_j >= 3) & (col_j < 6), 1.0, 0.0).astype(F32)[None, :]
    onek = jnp.where(col_j < 3, 1.0, 0.0).astype(F32)[None, :]
    return selq, selk, oneq, onek


def _cumaug(logf, batch, n_heads, *, tc):
    m = logf.shape[0]
    t = m // batch
    nt = t // tc
    w = n_heads * LANES
    tri = (jnp.arange(tc)[:, None] >= jnp.arange(tc)[None, :]).astype(BF16)
    selq, selk, oneq, onek = _aug_constants(n_heads)
    const = lambda b, i: (0, 0)
    row = lambda b, i: (b * nt + i, 0)
    return pl.pallas_call(
        _cumaug_kernel,
        grid=(batch, nt),
        in_specs=[
            pl.BlockSpec((tc, LANES), row),
            pl.BlockSpec((tc, tc), const),
            pl.BlockSpec((3 * LANES, w), const),
            pl.BlockSpec((3 * LANES, w), const),
            pl.BlockSpec((1, w), const),
            pl.BlockSpec((1, w), const),
        ],
        out_specs=[pl.BlockSpec((tc, w), row), pl.BlockSpec((tc, w), row)],
        out_shape=[jax.ShapeDtypeStruct((m, w), BF16), jax.ShapeDtypeStruct((m, w), BF16)],
        scratch_shapes=[pltpu.VMEM((8, LANES), F32)],
        compiler_params=_params(("parallel", "arbitrary"), 32),
    )(logf, tri, selq, selk, oneq, onek)


def _lambda(lq1, lk1, lq2, lk2, lam_init):
    a = jnp.sum(lq1[...] * lk1[...], axis=-1, keepdims=True)
    b = jnp.sum(lq2[...] * lk2[...], axis=-1, keepdims=True)
    return jnp.exp(a) - jnp.exp(b) + lam_init


def _flash_kernel(*refs, diff, tq, lam_init):
    if diff:
        (q_ref, k_ref, vt_ref, lq1, lk1, lq2, lk2, sub_ref,
         o_ref, q_sc, m_sc, acc_sc, s_sc) = refs
    else:
        (q_ref, qa_ref, k_ref, ka_ref, vt_ref,
         o_ref, q_sc, m_sc, acc_sc, s_sc) = refs
    n_tiles = q_ref.shape[0] // tq
    group = vt_ref.shape[0]
    cols = q_sc.shape[1]
    chunks = [slice(c * MXU_WIDTH, (c + 1) * MXU_WIDTH) for c in range(cols // MXU_WIDTH)]
    chunk_head = [(c * MXU_WIDTH) // tq % group for c in range(len(chunks))]
    ones = jnp.ones((SUM_ROWS, tq), BF16)
    if diff:
        lam = _lambda(lq1, lk1, lq2, lk2, lam_init)

    def head_lanes(g):
        return slice(g * LANES, (g + 1) * LANES)

    def prepare(i):
        par = i
        rows = slice(i * tq, (i + 1) * tq)
        if diff:
            q = q_ref[rows, :]
            lane = lax.broadcasted_iota(jnp.int32, q.shape, 1)
            zero = jnp.zeros_like(q)
            q_sc[par, 0:tq, :] = jnp.where(lane < LANES // 2, q, zero)
            q_sc[par, tq:, :] = jnp.where(lane < LANES // 2, zero, q)
        else:
            for g in range(group):
                q_sc[par, g * tq:(g + 1) * tq, 0:LANES] = q_ref[rows, head_lanes(g)]
                q_sc[par, g * tq:(g + 1) * tq, LANES:] = qa_ref[rows, head_lanes(g)]
        m_sc[par] = jnp.full(m_sc.shape[1:], NEG_INF, F32)
        acc_sc[par] = jnp.zeros(acc_sc.shape[1:], F32)

    def visible_keys(c, diagonal):
        return (c * MXU_WIDTH) % tq + MXU_WIDTH if diagonal else tq

    def scores(i, j, slot, diagonal=False):
        start = j * tq if isinstance(j, int) else pl.multiple_of(j * tq, tq)
        k = k_ref[pl.ds(start, tq), :]
        if diff:
            keys = [k]
        else:
            ka = ka_ref[pl.ds(start, tq), :]
            keys = [jnp.concatenate([k[:, head_lanes(g)], ka[:, head_lanes(g)]], axis=1)
                    for g in range(group)]
        for c, cs in enumerate(chunks):
            n = visible_keys(c, diagonal)
            s_sc[slot, 0:n, cs] = _dot_nt(keys[chunk_head[c]][0:n], q_sc[i, cs, :])

    def accumulate(i, j, slot, diagonal):
        par = i
        vts = [jnp.concatenate([vt_ref[g, j], ones], axis=0)
               for g in range(group)]
        for c, cs in enumerate(chunks):
            vt = vts[chunk_head[c]]
            n = visible_keys(c, diagonal)
            s = s_sc[slot, 0:n, cs]
            if diagonal:
                key = lax.broadcasted_iota(jnp.int32, s.shape, 0)
                qry = (lax.broadcasted_iota(jnp.int32, s.shape, 1) + c * MXU_WIDTH) % tq
                s = jnp.where(key <= qry, s, NEG_INF)
            m_prev = m_sc[par, :, cs]
            m_new = jnp.maximum(m_prev, jnp.max(s, axis=0, keepdims=True))
            alpha = jnp.exp2(m_prev - m_new)
            p = jnp.exp2(s - m_new).astype(BF16)
            acc_sc[par, :, cs] = alpha * acc_sc[par, :, cs] + _dot(vt[:, 0:n], p)
            m_sc[par, :, cs] = m_new

    def finalize(i):
        par = i
        out_t = acc_sc[par, 0:LANES, :] / acc_sc[par, LANES:LANES + 1, :]
        if diff:
            o = out_t[:, 0:tq] - lam * out_t[:, tq:]
            ms = jnp.mean(o * o, axis=0, keepdims=True)
            out_t = o * lax.rsqrt(ms + EPS) * sub_ref[...] * (1.0 - lam_init)
            o_ref[i * tq:(i + 1) * tq, :] = out_t.T.astype(o_ref.dtype)
        else:
            for g in range(group):
                o_ref[i * tq:(i + 1) * tq, head_lanes(g)] = (
                    out_t[:, g * tq:(g + 1) * tq].T.astype(o_ref.dtype))

    prepare(0)
    scores(0, 0, 0, diagonal=True)
    first = 0
    for i in range(n_tiles):
        a, b = first, 1 - first

        def pair(jj, carry, i=i, a=a, b=b):
            j = 2 * jj
            scores(i, j + 1, b)
            accumulate(i, j, a, False)
            scores(i, j + 2, a)
            accumulate(i, j + 1, b, False)
            return carry

        if i // 2:
            lax.fori_loop(0, i // 2, pair, 0)
        more = i + 1 < n_tiles
        if i % 2 == 0:
            if more:
                prepare(i + 1)
                scores(i + 1, 0, b)
            accumulate(i, i, a, True)
            first = b
        else:
            scores(i, i, b, diagonal=True)
            accumulate(i, i - 1, a, False)
            if more:
                prepare(i + 1)
                scores(i + 1, 0, a)
            accumulate(i, i, b, True)
            first = a
        finalize(i)


def _flash(q, k, vt, batch, extra, *, diff, tq, lam_init=0.0):
    m, w = q.shape
    n_heads = w // LANES
    t = m // batch
    nq = t // tq
    group = 1 if diff else 2
    seq = pl.BlockSpec((t, group * LANES), lambda b, h: (b, h))
    vspec = pl.BlockSpec((group, nq, LANES, tq), lambda b, h: (h, b, 0, 0))
    cols = 2 * tq
    if diff:
        vec = pl.BlockSpec((1, LANES // 2), lambda b, h: (0, 0))
        sub = pl.BlockSpec((LANES, 1), lambda b, h: (0, 0))
        in_specs = [seq, seq, vspec, vec, vec, vec, vec, sub]
        args = (q, k, vt) + tuple(extra)
        qw = LANES
    else:
        qa, ka = extra
        in_specs = [seq, seq, seq, seq, vspec]
        args = (q, qa, k, ka, vt)
        qw = 2 * LANES
    return pl.pallas_call(
        functools.partial(_flash_kernel, diff=diff, tq=tq, lam_init=lam_init),
        grid=(batch, n_heads // group),
        in_specs=in_specs,
        out_specs=seq,
        out_shape=jax.ShapeDtypeStruct((m, w), BF16),
        scratch_shapes=[
            pltpu.VMEM((nq, cols, qw), BF16),
            pltpu.VMEM((nq, 1, cols), F32),
            pltpu.VMEM((nq, LANES + SUM_ROWS, cols), F32),
            pltpu.VMEM((2, tq, cols), F32),
        ],
        compiler_params=_params(("parallel", "parallel"), 48),
    )(*args)


def _pastsum_kernel(pt_ref, logf_hbm, u_ref, ubig_ref, r_out, buf, sem, *, layer, n_pages):
    b = pl.program_id(0)

    def copy(p):
        page = pt_ref[b * n_pages + p]
        return pltpu.make_async_copy(logf_hbm.at[layer, page], buf.at[p], sem)

    def start(p, c):
        copy(p).start()
        return c

    def wait(p, c):
        copy(p).wait()
        return c

    lax.fori_loop(0, n_pages, start, 0)
    lax.fori_loop(0, n_pages, wait, 0)

    xt = buf[...].reshape(r_out.shape)
    within = _dot3(xt, u_ref[...])
    later = jnp.sum(_dot3_left(ubig_ref[...], xt), axis=1, keepdims=True)
    r_out[...] = within + later


def _pastsum(cache_logf_t, page_table_flat, layer, dec_batch):
    n_pages = page_table_flat.shape[0] // dec_batch
    n_heads, t = cache_logf_t.shape[2:]
    u = (jnp.arange(t)[:, None] > jnp.arange(t)[None, :]).astype(BF16)
    r = jnp.arange(n_pages * n_heads)
    ubig = ((r[None, :] % n_heads == r[:, None] % n_heads)
            & (r[None, :] // n_heads > r[:, None] // n_heads)).astype(BF16)
    rows = n_pages * n_heads
    const = lambda b, pt: (0, 0)
    grid_spec = pltpu.PrefetchScalarGridSpec(
        num_scalar_prefetch=1,
        grid=(dec_batch,),
        in_specs=[
            pl.BlockSpec(memory_space=pl.ANY),
            pl.BlockSpec((t, t), const),
            pl.BlockSpec((rows, rows), const),
        ],
        out_specs=pl.BlockSpec((None, rows, t), lambda b, pt: (b, 0, 0)),
        scratch_shapes=[pltpu.VMEM((n_pages, n_heads, t), F32), pltpu.SemaphoreType.DMA(())],
    )
    return pl.pallas_call(
        functools.partial(_pastsum_kernel, layer=layer, n_pages=n_pages),
        grid_spec=grid_spec,
        out_shape=jax.ShapeDtypeStruct((dec_batch, rows, t), F32),
        compiler_params=_params(("arbitrary",), 32),
    )(page_table_flat, cache_logf_t, u, ubig)


def _decode_kernel(*refs, diff, n_steps, group, n_heads, t_new, page, lam_init):
    refs = list(refs)
    pt_ref, q_ref = refs[0], refs[1]
    k_refs = refs[2:2 + group]
    v_refs = refs[2 + group:2 + 2 * group]
    rest = refs[2 + 2 * group:]
    if diff:
        (kn_ref, vn_ref, lq1, lk1, lq2, lk2, sub_ref,
         o_ref, q_sc, m_sc, l_sc, acc_sc) = rest
    else:
        (kn_ref, vn_ref, r_ref, lw_ref,
         o_ref, q_sc, m_sc, l_sc, acc_sc, qc_sc) = rest
    p = pl.program_id(1)
    rows = q_sc.shape[0]
    groups = 2 * n_heads if diff else n_heads
    group_w = q_sc.shape[1] // groups

    def q_index(shape):
        return lax.broadcasted_iota(jnp.int32, shape, 0) % t_new

    def page_rows(ref):
        heads = [ref[pl.ds(h, page, stride=n_heads), :] for h in range(n_heads)]
        return jnp.concatenate(heads, axis=1).astype(BF16)

    @pl.when(p == 0)
    def _():
        q = jnp.concatenate([q_ref[...]] * (rows // t_new), axis=0)
        r = lax.broadcasted_iota(jnp.int32, q.shape, 0)
        lane = lax.broadcasted_iota(jnp.int32, q.shape, 1)
        if diff:
            grp = ((r // t_new) % n_heads) * 2 + r // (t_new * n_heads)
        else:
            grp = r // t_new
        q_sc[...] = jnp.where(lane // group_w == grp, q, 0.0).astype(BF16)
        m_sc[...] = jnp.full_like(m_sc, NEG_INF)
        l_sc[...] = jnp.zeros_like(l_sc)
        acc_sc[...] = jnp.zeros_like(acc_sc)
        if not diff:
            lw = lw_ref[...]
            i_idx = lax.broadcasted_iota(jnp.int32, lw.shape, 1)
            qc = jnp.sum(jnp.where(i_idx <= q_index(lw.shape), lw, 0.0), axis=1, keepdims=True)
            qc_sc[...] = qc

    def update(s, values):
        m_prev = m_sc[...]
        m_new = jnp.maximum(m_prev, jnp.max(s, axis=-1, keepdims=True))
        alpha = jnp.exp(m_prev - m_new)
        pr = jnp.exp(s - m_new)
        l_sc[...] = alpha * l_sc[...] + jnp.sum(pr, axis=-1, keepdims=True)
        pr = pr.astype(BF16)
        pv = _dot(pr[:, 0:page], values[0])
        for g in range(1, len(values)):
            pv = pv + _dot(pr[:, g * page:(g + 1) * page], values[g])
        acc_sc[...] = alpha * acc_sc[...] + pv
        m_sc[...] = m_new

    @pl.when(p < n_steps)
    def _():
        parts = []
        for g in range(group):
            if diff:
                s = _dot(q_sc[...], k_refs[g][...].astype(BF16))
            else:
                s = _dot_nt(q_sc[...], page_rows(k_refs[g]))
                r8 = r_ref[g * n_heads:(g + 1) * n_heads, :]
                bias = jnp.concatenate(
                    [jnp.broadcast_to(r8[h:h + 1, :], (t_new, page)) for h in range(n_heads)],
                    axis=0)
                s = s + (qc_sc[...] + bias)
            parts.append(s)
        update(jnp.concatenate(parts, axis=1), [page_rows(v) for v in v_refs])

    @pl.when(p == n_steps)
    def _():
        s = _dot_nt(q_sc[...], kn_ref[...].astype(BF16))
        col = lax.broadcasted_iota(jnp.int32, s.shape, 1)
        qi = q_index(s.shape)
        if not diff:
            lw = lw_ref[...]
            bias = jnp.zeros(s.shape, F32)
            for i in range(t_new):
                bias = bias + jnp.where((qi >= i) & (col < i), lw[:, i:i + 1], 0.0)
            s = s + bias
        s = jnp.where(col <= qi, s, NEG_INF)
        update(s, [vn_ref[...].astype(BF16)])

        acc = acc_sc[...]
        inv_l = 1.0 / l_sc[...]
        if diff:
            lam = _lambda(lq1, lk1, lq2, lk2, lam_init)
            half = n_heads * t_new
        for h in range(n_heads):
            lanes = slice(h * LANES, (h + 1) * LANES)
            r0 = h * t_new
            o = acc[r0:r0 + t_new, lanes] * inv_l[r0:r0 + t_new, :]
            if diff:
                o2 = acc[half + r0:half + r0 + t_new, lanes] * inv_l[half + r0:half + r0 + t_new, :]
                o = _rms_rows(o - lam * o2, sub_ref[...]) * (1.0 - lam_init)
            o_ref[:, lanes] = o


def _decode(q, cache_k, cache_v, k_new, v_new, page_table_flat, layer, dec_batch, extra,
            *, diff, group, lam_init=0.0):
    w = q.shape[1]
    t_new = q.shape[0] // dec_batch
    page = k_new.shape[1]
    n_pages = page_table_flat.shape[0] // dec_batch
    n_steps = n_pages // group
    n_heads = w // LANES
    rows = (2 if diff else 1) * n_heads * t_new

    def page_map(g):
        def index(b, p, pt):
            return (layer, pt[b * n_pages + jnp.minimum(p, n_steps - 1) * group + g], 0, 0)
        return index

    per_b = lambda b, p, pt: (b, 0)
    per_b3 = lambda b, p, pt: (b, 0, 0)
    kblock = (None, None) + cache_k.shape[2:]
    vblock = (None, None) + cache_v.shape[2:]
    in_specs = [pl.BlockSpec((t_new, w), per_b)]
    in_specs += [pl.BlockSpec(kblock, page_map(g)) for g in range(group)]
    in_specs += [pl.BlockSpec(vblock, page_map(g)) for g in range(group)]
    in_specs += [pl.BlockSpec((None, page, w), per_b3), pl.BlockSpec((None, page, w), per_b3)]
    scratch = [
        pltpu.VMEM((rows, w), BF16),
        pltpu.VMEM((rows, 1), F32),
        pltpu.VMEM((rows, 1), F32),
        pltpu.VMEM((rows, w), F32),
    ]
    if diff:
        vec = pl.BlockSpec((1, LANES // 2), lambda b, p, pt: (0, 0))
        in_specs += [vec, vec, vec, vec, pl.BlockSpec((1, LANES), lambda b, p, pt: (0, 0))]
    else:
        in_specs += [
            pl.BlockSpec((None, group * n_heads, page),
                         lambda b, p, pt: (b, jnp.minimum(p, n_steps - 1), 0)),
            pl.BlockSpec((None, rows, t_new), per_b3),
        ]
        scratch.append(pltpu.VMEM((rows, 1), F32))
    grid_spec = pltpu.PrefetchScalarGridSpec(
        num_scalar_prefetch=1,
        grid=(dec_batch, n_steps + 1),
        in_specs=in_specs,
        out_specs=pl.BlockSpec((t_new, w), per_b),
        scratch_shapes=scratch,
    )
    return pl.pallas_call(
        functools.partial(_decode_kernel, diff=diff, n_steps=n_steps, group=group,
                          n_heads=n_heads, t_new=t_new, page=page, lam_init=lam_init),
        grid_spec=grid_spec,
        out_shape=jax.ShapeDtypeStruct((dec_batch * t_new, w), F32),
        compiler_params=_params(("parallel", "arbitrary"), 48),
    )(page_table_flat, q, *([cache_k] * group), *([cache_v] * group), k_new, v_new, *extra)


def _oproj_kernel(x_ref, a_ref, b_ref, wa_ref, wb_ref, o_ref):
    o_ref[...] = (x_ref[...] + _dot(a_ref[...].astype(BF16), wa_ref[...])
                  + _dot(b_ref[...].astype(BF16), wb_ref[...]))


def _oproj(x, a, b, w_o, layer, *, tm):
    m, d = x.shape
    w = a.shape[1]
    row = lambda i: (i, 0)
    return pl.pallas_call(
        _oproj_kernel,
        grid=(m // tm,),
        in_specs=[
            pl.BlockSpec((tm, d), row),
            pl.BlockSpec((tm, w), row),
            pl.BlockSpec((tm, w), row),
            pl.BlockSpec((None, w, d), lambda i: (layer, 0, 0)),
            pl.BlockSpec((None, w, d), lambda i: (layer, 1, 0)),
        ],
        out_specs=pl.BlockSpec((tm, d), row),
        out_shape=jax.ShapeDtypeStruct((m, d), F32),
        compiler_params=_params(("parallel",), 48),
    )(x, a, b, w_o, w_o)


def _mlp_kernel(x_ref, g_ref, wu_ref, wd_ref, o_ref, xn_sc, acc_sc):
    j = pl.program_id(1)

    @pl.when(j == 0)
    def _():
        x = x_ref[...]
        xn_sc[...] = _rms_rows(x, g_ref[...]).astype(BF16)
        acc_sc[...] = x

    h = jnp.maximum(_dot(xn_sc[...], wu_ref[...]), 0.0)
    acc_sc[...] += _dot((h * h).astype(BF16), wd_ref[...])

    @pl.when(j == pl.num_programs(1) - 1)
    def _():
        o_ref[...] = acc_sc[...]


def _mlp(x, g, wu, wd, layer, *, tm, tf):
    m, d = x.shape
    f = wu.shape[2]
    row = lambda i, j: (i, 0)
    return pl.pallas_call(
        _mlp_kernel,
        grid=(m // tm, f // tf),
        in_specs=[
            pl.BlockSpec((tm, d), row),
            pl.BlockSpec((1, d), lambda i, j: (0, 0)),
            pl.BlockSpec((None, d, tf), lambda i, j: (layer, 0, j)),
            pl.BlockSpec((None, tf, d), lambda i, j: (layer, j, 0)),
        ],
        out_specs=pl.BlockSpec((tm, d), row),
        out_shape=jax.ShapeDtypeStruct((m, d), F32),
        scratch_shapes=[pltpu.VMEM((tm, d), BF16), pltpu.VMEM((tm, d), F32)],
        compiler_params=_params(("parallel", "arbitrary"), 56),
    )(x, g, wu, wd)


def _rope_tables(pos, dk):
    inv_freq = ROPE_THETA ** (-jnp.arange(0, dk, 2, dtype=F32) / dk)
    ang = pos.astype(F32)[:, None] * inv_freq[None, :]
    cos, sin = jnp.cos(ang), jnp.sin(ang)
    reps = LANES // dk
    cos = jnp.tile(jnp.concatenate([cos, cos], axis=-1), (1, reps))
    sin = jnp.tile(jnp.concatenate([-sin, sin], axis=-1), (1, reps))
    return cos, sin


def _largest_divisor(n, cap):
    return max(d for d in range(1, cap + 1) if n % d == 0)


def kernel(x_prompt, x_sample, cache_fox_k, cache_fox_v, cache_fox_logf, cache_diff_k, cache_diff_v, page_table, norm_attn, w_in, b_forget, fox_q_norm, fox_k_norm, diff_q_norm, diff_k_norm, lambda_q1, lambda_k1, lambda_q2, lambda_k2, diff_subln, w_o, norm_mlp, w_up, w_down):
    depth = w_in.shape[0]
    batch, seq, d_model = x_prompt.shape
    dec_batch, dec_seq, _ = x_sample.shape
    _, n_phys, page, fox_heads, fox_dim = cache_fox_k.shape
    diff_heads, diff_dim = cache_diff_v.shape[3], cache_diff_v.shape[4]
    dk = cache_diff_k.shape[5]
    fox_w = fox_heads * fox_dim
    diff_w = diff_heads * diff_dim
    n_pages = page_table.shape[1]
    past_len = n_pages * page
    assert fox_dim == LANES and diff_dim == LANES and 2 * dk == LANES
    assert seq % ROW_TILE == 0

    c1 = 3 * fox_w + fox_heads
    w_in_b = jnp.transpose(w_in, (0, 2, 1)).astype(BF16)
    b_gate = jnp.pad(b_forget, ((0, 0), (0, LANES - fox_heads)))[:, None, :]
    w_diff = w_in_b[:, c1:, :]
    assert fox_w == diff_w
    w_o_b = w_o.astype(BF16)
    w_up_b = w_up.astype(BF16)
    w_down_b = w_down.astype(BF16)
    dqn = jnp.tile(diff_q_norm, (1, 2))[:, None, :]
    dkn = jnp.tile(diff_k_norm, (1, 2))[:, None, :]

    pt_flat = page_table.reshape(-1)
    cache_fk = cache_fox_k.reshape(depth, n_phys, page * fox_heads, fox_dim)
    cache_fv = cache_fox_v.reshape(depth, n_phys, page * fox_heads, fox_dim)
    cache_dv = cache_diff_v.reshape(depth, n_phys, page * diff_heads, diff_dim)
    cache_dk = jnp.transpose(cache_diff_k, (0, 1, 3, 4, 5, 2)).reshape(depth, n_phys, diff_w, page)
    cache_lf = jnp.transpose(cache_fox_logf, (0, 1, 3, 2))
    group = _largest_divisor(n_pages, PAGES_PER_STEP)

    fox_scale = fox_dim ** -0.5
    diff_scale = dk ** -0.5

    def trunk(x, pos, decode):
        m = x.shape[0]
        n_seq = dec_batch if decode else batch
        t = m // n_seq
        tm = m if decode else ROW_TILE
        cos, sin = _rope_tables(pos, dk)
        if decode:
            cos, sin = jnp.tile(cos, (n_seq, 1)), jnp.tile(sin, (n_seq, 1))
        q_unit = 1.0 if decode else LOG2E
        news = []
        fox_stacked, diff_stacked = (), ()
        for l in range(depth):
            lam_init = 0.8 - 0.6 * math.exp(-0.3 * l)
            lam_vecs = (lambda_q1[l][None], lambda_k1[l][None], lambda_q2[l][None],
                        lambda_k2[l][None])
            place = dict(final=not decode, batch=n_seq, depth=depth, layer=l)
            fox = _proj("fox", x, norm_attn[l][None], w_in_b, fox_q_norm[l][None],
                        fox_k_norm[l][None], (w_in_b, b_gate[l]), tm=tm, w=fox_w,
                        scale=fox_scale * q_unit, prev=fox_stacked, **place)
            dif = _proj("diff", x, norm_attn[l][None], w_diff, dqn[l], dkn[l], (cos, sin),
                        tm=tm, scale=diff_scale * q_unit, prev=diff_stacked, **place)
            logf = fox[3]
            if decode:
                fq, fk, fv = fox[:3]
                dq, dkk, dv = dif[:3]
                logf_h = logf[:, :fox_heads]
                pad = lambda a: jnp.pad(a.reshape(n_seq, t, -1), ((0, 0), (0, page - t), (0, 0)))
                rsum = _pastsum(cache_lf, pt_flat, l, n_seq)
                lw = jnp.repeat(logf_h.reshape(n_seq, t, fox_heads).transpose(0, 2, 1), t, axis=1)
                fox_o = _decode(fq, cache_fk, cache_fv, pad(fk), pad(fv), pt_flat, l, n_seq,
                                (rsum, lw), diff=False, group=group)
                diff_o = _decode(dq, cache_dk, cache_dv, pad(dkk), pad(dv), pt_flat, l, n_seq,
                                 lam_vecs + (diff_subln[l][None],), diff=True, group=group,
                                 lam_init=lam_init)
                news.append((fk, fv, logf_h, dkk, dv))
            else:
                fq, fkb, fvt = fox[:3]
                dq, dkb, dvt = dif[:3]
                fox_stacked, diff_stacked = tuple(fox[4:]), tuple(dif[3:])
                qa, ka = _cumaug(logf, n_seq, fox_heads, tc=ROW_TILE)
                fox_o = _flash(fq, fkb, fvt, n_seq, (qa, ka), diff=False, tq=ROW_TILE)
                diff_o = _flash(dq, dkb, dvt, n_seq, lam_vecs + (diff_subln[l][:, None],),
                                diff=True, tq=ROW_TILE, lam_init=lam_init)
            x = _oproj(x, fox_o, diff_o, w_o_b, l, tm=tm)
            x = _mlp(x, norm_mlp[l][None], w_up_b, w_down_b, l, tm=tm, tf=MLP_FF_TILE)
        lead = (depth, n_seq, t)
        if decode:
            stack = lambda i, shape: jnp.stack([n[i] for n in news]).reshape(lead + shape)
            new_cache = (stack(0, (fox_heads, fox_dim)), stack(1, (fox_heads, fox_dim)),
                         stack(2, (fox_heads,)), stack(3, (diff_heads, 2, dk)),
                         stack(4, (diff_heads, diff_dim)))
        else:
            fk, fv, logf_t = fox_stacked
            dk_t, dv = diff_stacked
            new_cache = (fk.reshape(lead + (fox_heads, fox_dim)),
                         fv.reshape(lead + (fox_heads, fox_dim)),
                         jnp.transpose(logf_t, (0, 1, 3, 2)),
                         jnp.transpose(dk_t.reshape(depth, n_seq, diff_heads, 2, dk, t),
                                       (0, 1, 5, 2, 3, 4)),
                         dv.reshape(lead + (diff_heads, diff_dim)))
        return (x.reshape(n_seq, t, d_model),) + new_cache

    pos_prompt = jnp.arange(seq, dtype=jnp.int32)
    pos_sample = past_len + jnp.arange(dec_seq, dtype=jnp.int32)
    yp = trunk(x_prompt.reshape(batch * seq, d_model), pos_prompt, False)
    ys = trunk(x_sample.reshape(dec_batch * dec_seq, d_model), pos_sample, True)
    return (yp[0], ys[0]) + yp[1:] + ys[1:]
```
